```python
import jax, jax.numpy as jnp
from jax import lax
import numpy as np

D_MODEL = 1024
BATCH = 8
SEQ = 4096
DEPTH = 1

GRID_W = 64
CTX_LEN = 256
N_HEADS_A = 8
HEAD_DIM_A = 128
WIDTH_A = N_HEADS_A * HEAD_DIM_A
QKV_CONV = 3
CHUNK = 64
WIDTH_B = 1024
N_GROUPS_B = 8
CONV_B = 3
N_DIR = 2
SPLIT_SIZES = (3 * WIDTH_A, WIDTH_A, 2 * N_DIR * N_HEADS_A, WIDTH_B, WIDTH_B, WIDTH_B, WIDTH_B, D_MODEL, D_MODEL)
IN_COLS = sum(SPLIT_SIZES)
DN_ALPHA = (2.0 * DEPTH) ** 0.25
DN_BETA = (8.0 * DEPTH) ** -0.25
LN_EPS = 1e-5
RMS_EPS = 1e-6
L2_EPS = 1e-6

kernel_name = "hybrid_gdn_shortconv_dit_block"


def _layernorm(x, gain=None, bias=None):
    xf = x.astype(jnp.float32)
    mu = jnp.mean(xf, axis=-1, keepdims=True)
    var = jnp.mean(jnp.square(xf - mu), axis=-1, keepdims=True)
    y = (xf - mu) * lax.rsqrt(var + LN_EPS)
    if gain is not None:
        y = y * gain.astype(jnp.float32) + bias.astype(jnp.float32)
    return y.astype(x.dtype)


def _conv_seq(u, w):
    k = w.shape[0]
    r = k // 2
    length = u.shape[1]
    up = jnp.pad(u, ((0, 0), (r, r), (0, 0)))
    out = up[:, 0:length] * w[0]
    for i in range(1, k):
        out = out + up[:, i:i + length] * w[i]
    return out


def _conv_latent(u, w):
    b, length, ch = u.shape
    rows = length // GRID_W
    y = _conv_seq(u.reshape(b * rows, GRID_W, ch), w)
    return y.reshape(b, length, ch)


def _split_cols(p):
    idx = [int(i) for i in np.cumsum(SPLIT_SIZES)[:-1]]
    return jnp.split(p, idx, axis=-1)


def _heads(t):
    b, length, _ = t.shape
    return t.reshape(b, length, N_HEADS_A, HEAD_DIM_A).transpose(0, 2, 1, 3)


def _l2norm(t):
    tf = t.astype(jnp.float32)
    return tf * lax.rsqrt(jnp.sum(tf * tf, axis=-1, keepdims=True) + L2_EPS)


def _gdn_inputs(qkv, ab, conv_w, conv_fn):
    qkv = jax.nn.silu(conv_fn(qkv, conv_w))
    q, k, v = jnp.split(qkv, 3, axis=-1)
    q = _l2norm(_heads(q)) * (HEAD_DIM_A ** -0.5)
    k = _l2norm(_heads(k))
    v = _heads(v).astype(jnp.float32)
    ab = jnp.transpose(ab.astype(jnp.float32), (0, 2, 1))
    a_logit, b_logit = jnp.split(ab, 2, axis=1)
    return q, k, v, a_logit, b_logit


def _gdn_chunked(q, k, v, g, beta, s0):
    b, h, length, dk = q.shape
    dv = v.shape[-1]
    n = length // CHUNK
    q = q.reshape(b, h, n, CHUNK, dk)
    k = k.reshape(b, h, n, CHUNK, dk)
    v = v.reshape(b, h, n, CHUNK, dv)
    g_cum = jnp.cumsum(g.reshape(b, h, n, CHUNK), axis=-1)
    beta = beta.reshape(b, h, n, CHUNK)
    tri_incl = jnp.tril(jnp.ones((CHUNK, CHUNK), dtype=bool))
    tri_strict = jnp.tril(jnp.ones((CHUNK, CHUNK), dtype=bool), k=-1)
    diff = g_cum[..., :, None] - g_cum[..., None, :]
    decay = jnp.exp(jnp.where(tri_incl, diff, -jnp.inf))
    k_beta = k * beta[..., None]
    v_beta = v * beta[..., None]
    m = jnp.where(tri_strict, jnp.einsum('bhncd,bhnsd->bhncs', k_beta, k) * decay, 0.0)
    eye = jnp.eye(CHUNK, dtype=jnp.float32)
    t_inv = lax.linalg.triangular_solve(eye + m, jnp.broadcast_to(eye, m.shape), left_side=True, lower=True)
    u = jnp.einsum('bhncs,bhnsv->bhncv', t_inv, v_beta)
    w = jnp.einsum('bhncs,bhnsd->bhncd', t_inv, k_beta * jnp.exp(g_cum)[..., None])
    qk = jnp.einsum('bhncd,bhnsd->bhncs', q, k) * decay
    q_dec = q * jnp.exp(g_cum)[..., None]
    k_dec = k * jnp.exp(g_cum[..., -1:] - g_cum)[..., None]
    chunk_decay = jnp.exp(g_cum[..., -1])

    def step(s, xs):
        qk_i, q_dec_i, k_dec_i, u_i, w_i, d_i = xs
        v_new = u_i - jnp.einsum('bhck,bhkv->bhcv', w_i, s)
        o = jnp.einsum('bhck,bhkv->bhcv', q_dec_i, s) + jnp.einsum('bhcs,bhsv->bhcv', qk_i, v_new)
        s = s * d_i[..., None, None] + jnp.einsum('bhck,bhcv->bhkv', k_dec_i, v_new)
        return s, o

    xs = tuple(jnp.moveaxis(t, 2, 0) for t in (qk, q_dec, k_dec, u, w, chunk_decay))
    s_final, o = lax.scan(step, s0, xs)
    o = jnp.moveaxis(o, 0, 2).reshape(b, h, length, dv)
    return s_final, o


def _direction_inputs(inp, a_rate_d, dtb_d, d, reverse):
    q, k, v, a_logit, b_logit = inp
    hs = slice(d * N_HEADS_A, (d + 1) * N_HEADS_A)
    g = -a_rate_d[:, None] * jax.nn.softplus(a_logit[:, hs] + dtb_d[:, None])
    beta = jax.nn.sigmoid(b_logit[:, hs])
    seqs = (q, k, v, g, beta)
    if reverse:
        seqs = tuple(jnp.flip(t, axis=2) for t in seqs)
    return seqs


def _gdn_bidir(in_c, in_x, a_log, dt_bias):
    a_rate = jnp.exp(a_log.astype(jnp.float32))
    dtb = dt_bias.astype(jnp.float32)
    b = in_x[0].shape[0]
    outs_c, outs_x = [], []
    for d in range(N_DIR):
        reverse = d == 1
        s0 = jnp.zeros((b, N_HEADS_A, HEAD_DIM_A, HEAD_DIM_A), jnp.float32)
        s_c, o_c = _gdn_chunked(*_direction_inputs(in_c, a_rate[d], dtb[d], d, reverse), s0)
        _, o_x = _gdn_chunked(*_direction_inputs(in_x, a_rate[d], dtb[d], d, reverse), s_c)
        if reverse:
            o_c = jnp.flip(o_c, axis=2)
            o_x = jnp.flip(o_x, axis=2)
        outs_c.append(o_c)
        outs_x.append(o_x)
    return outs_c[0] + outs_c[1], outs_x[0] + outs_x[1]


def _gated_rmsnorm(o, z, w):
    o = jnp.transpose(o, (0, 2, 1, 3))
    o = o * lax.rsqrt(jnp.mean(o * o, axis=-1, keepdims=True) + RMS_EPS) * w.astype(jnp.float32)
    y = o * jax.nn.silu(z.reshape(o.shape).astype(jnp.float32))
    return y.reshape(o.shape[0], o.shape[1], WIDTH_A).astype(z.dtype)


def _branch_merge(o_a, p, conv_fn, o_norm_w, conv_b_w, conv_b_b, w_a, w_b, w_out):
    _, z_a, _, x_in, b_gate, c_gate, z_b, g_a, g_b = p
    y_a = _gated_rmsnorm(o_a, z_a, o_norm_w)
    y_b = b_gate * (conv_fn(c_gate * x_in, conv_b_w) + conv_b_b) * jax.nn.silu(z_b)
    merged = jax.nn.sigmoid(g_a) * (y_a @ w_a) + jax.nn.sigmoid(g_b) * (y_b @ w_b)
    return merged @ w_out


def setup_inputs(seed: int = 0) -> dict:
    key = jax.random.key(seed)
    ks = jax.random.split(key, 20)
    f32 = jnp.float32
    nrm = lambda k, shape, s: jax.random.normal(k, shape, f32) * s
    dt = jnp.exp(jax.random.uniform(ks[9], (DEPTH, N_DIR, N_HEADS_A), f32, np.log(1e-3), np.log(1e-1)))
    return {
        "x": nrm(ks[0], (BATCH, SEQ, D_MODEL), 1.0),
        "c": nrm(ks[1], (BATCH, D_MODEL), 1.0),
        "ctx": nrm(ks[2], (BATCH, CTX_LEN, D_MODEL), 1.0),
        "c_ctx": nrm(ks[3], (D_MODEL,), 1.0),
        "w_mod": nrm(ks[4], (DEPTH, D_MODEL, 3 * D_MODEL), 0.5 * D_MODEL ** -0.5),
        "b_mod": nrm(ks[5], (DEPTH, 3 * D_MODEL), 0.01),
        "w_in": nrm(ks[6], (DEPTH, D_MODEL, IN_COLS), D_MODEL ** -0.5),
        "b_in": nrm(ks[7], (DEPTH, IN_COLS), 0.01),
        "conv_qkv_w": nrm(ks[8], (DEPTH, QKV_CONV, 3 * WIDTH_A), QKV_CONV ** -0.5),
        "a_log": jnp.log(jax.random.uniform(ks[10], (DEPTH, N_DIR, N_HEADS_A), f32, 1.0, 16.0)),
        "dt_bias": dt + jnp.log(-jnp.expm1(-dt)),
        "o_norm_w": 1.0 + nrm(ks[11], (DEPTH, HEAD_DIM_A), 0.01),
        "conv_b_w": nrm(ks[12], (DEPTH, CONV_B, WIDTH_B), CONV_B ** -0.5),
        "conv_b_b": nrm(ks[13], (DEPTH, WIDTH_B), 0.01),
        "w_a": nrm(ks[14], (DEPTH, WIDTH_A, D_MODEL), DN_BETA * WIDTH_A ** -0.5),
        "w_b": nrm(ks[15], (DEPTH, WIDTH_B, D_MODEL), DN_BETA * WIDTH_B ** -0.5),
        "w_out": nrm(ks[16], (DEPTH, D_MODEL, D_MODEL), DN_BETA * D_MODEL ** -0.5),
        "ln_g": 1.0 + nrm(ks[17], (DEPTH, D_MODEL), 0.01),
        "ln_b": nrm(ks[18], (DEPTH, D_MODEL), 0.01),
    }


def reference(x, c, ctx, c_ctx, w_mod, b_mod, w_in, b_in, conv_qkv_w, a_log, dt_bias, o_norm_w,
              conv_b_w, conv_b_b, w_a, w_b, w_out, ln_g, ln_b):
    for l in range(DEPTH):
        mod_x = jax.nn.silu(c) @ w_mod[l] + b_mod[l]
        mod_c = jax.nn.silu(c_ctx) @ w_mod[l] + b_mod[l]
        shift_x, scale_x, gate_x = jnp.split(mod_x[:, None, :], 3, axis=-1)
        shift_c, scale_c, gate_c = jnp.split(mod_c, 3, axis=-1)
        hx = _layernorm(x) * (1.0 + scale_x) + shift_x
        hc = _layernorm(ctx) * (1.0 + scale_c) + shift_c
        px = _split_cols(hx @ w_in[l] + b_in[l])
        pc = _split_cols(hc @ w_in[l] + b_in[l])
        in_x = _gdn_inputs(px[0], px[2], conv_qkv_w[l], _conv_latent)
        in_c = _gdn_inputs(pc[0], pc[2], conv_qkv_w[l], _conv_seq)
        o_a_c, o_a_x = _gdn_bidir(in_c, in_x, a_log[l], dt_bias[l])
        y_x = _branch_merge(o_a_x, px, _conv_latent, o_norm_w[l], conv_b_w[l], conv_b_b[l], w_a[l], w_b[l], w_out[l])
        if l < DEPTH - 1:
            y_c = _branch_merge(o_a_c, pc, _conv_seq, o_norm_w[l], conv_b_w[l], conv_b_b[l], w_a[l], w_b[l], w_out[l])
            ctx = _layernorm(DN_ALPHA * ctx + gate_c * y_c, ln_g[l], ln_b[l])
        x = _layernorm(DN_ALPHA * x + gate_x * y_x, ln_g[l], ln_b[l])
    return x
```

```python
import functools

import jax
import jax.numpy as jnp
from jax import lax
from jax.experimental import pallas as pl
from jax.experimental.pallas import tpu as pltpu

D_MODEL = 1024
N_HEADS = 8
HEAD_DIM = 128
WIDTH_A = N_HEADS * HEAD_DIM
WIDTH_B = 1024
N_DIR = 2
CHUNK = 64
GRID_W = 64
DEPTH = 1
DN_ALPHA = (2.0 * DEPTH) ** 0.25
LN_EPS = 1e-5
RMS_EPS = 1e-6
L2_EPS = 1e-6

GROUP = 256
CHUNKS_PER_GROUP = GROUP // CHUNK
LANES = 128
N_AB = N_DIR * N_HEADS

PROJ_TILE = 512
PROJ_COLS = 512
MERGE_TILE = 256
MERGE_COLS = 256
MOD_COLS = 512
MIB = 1024 * 1024

F32 = jnp.float32
BF16 = jnp.bfloat16


def _dot(a, b):
    return jnp.dot(a, b, preferred_element_type=F32)


def _split3(a):
    hi = a.astype(BF16)
    r = a - hi.astype(F32)
    mid = r.astype(BF16)
    lo = (r - mid.astype(F32)).astype(BF16)
    return hi, mid, lo


def _silu(z):
    return z * jax.nn.sigmoid(z)


def _layernorm(x):
    mu = jnp.mean(x, axis=-1, keepdims=True)
    xc = x - mu
    var = jnp.mean(xc * xc, axis=-1, keepdims=True)
    return xc * lax.rsqrt(var + LN_EPS)


def _row_conv3(u, w, row_len):
    n = u.shape[0]
    pos = lax.broadcasted_iota(jnp.int32, (n, 1), 0) & (row_len - 1)
    has_prev = (pos != 0).astype(F32)
    has_next = (pos != row_len - 1).astype(F32)
    prev = pltpu.roll(u, 1, 0) * has_prev
    nxt = pltpu.roll(u, n - 1, 0) * has_next
    return prev * w[0:1, :] + u * w[1:2, :] + nxt * w[2:3, :]


def _mod_kernel(c_ref, w_ref, b_ref, o_ref):
    s = _silu(c_ref[...])
    s_hi, s_mid, _ = _split3(s)
    w = w_ref[...]
    w_hi, w_mid, _ = _split3(w)
    o_ref[...] = _dot(s_hi, w_hi) + _dot(s_hi, w_mid) + _dot(s_mid, w_hi) + b_ref[...]


def _modulation(c16, w_mod, b_mod):
    n_out = w_mod.shape[1]
    return pl.pallas_call(
        _mod_kernel,
        grid=(n_out // MOD_COLS,),
        in_specs=[
            pl.BlockSpec((16, D_MODEL), lambda j: (0, 0)),
            pl.BlockSpec((D_MODEL, MOD_COLS), lambda j: (0, j)),
            pl.BlockSpec((1, MOD_COLS), lambda j: (0, j)),
        ],
        out_specs=pl.BlockSpec((16, MOD_COLS), lambda j: (0, j)),
        out_shape=jax.ShapeDtypeStruct((16, n_out), F32),
        name="mod",
    )(c16, w_mod, b_mod)


def _proj_kernel(x_ref, mod_ref, wqkv_ref, bqkv_ref, wab_ref, bab_ref, cw_ref, par_ref,
                 q_ref, k_ref, v_ref, gb_ref, *, tile, row_len):
    m = mod_ref[...]
    h = _layernorm(x_ref[...]) * (1.0 + m[1:2, :]) + m[0:1, :]
    hb = h.astype(BF16)

    heads_per_pass = PROJ_COLS // HEAD_DIM
    outs = (q_ref, k_ref, v_ref)
    for j in range(3 * WIDTH_A // PROJ_COLS):
        cols = slice(j * PROJ_COLS, (j + 1) * PROJ_COLS)
        p = _dot(hb, wqkv_ref[:, cols]) + bqkv_ref[:, cols]
        s = _silu(_row_conv3(p, cw_ref[:, cols], row_len))
        which = (j * PROJ_COLS) // WIDTH_A
        head0 = ((j * PROJ_COLS) % WIDTH_A) // HEAD_DIM
        for hh in range(heads_per_pass):
            sl = s[:, hh * HEAD_DIM:(hh + 1) * HEAD_DIM]
            if which < 2:
                sl = sl * lax.rsqrt(jnp.sum(sl * sl, axis=-1, keepdims=True) + L2_EPS)
            if which == 0:
                sl = sl * (HEAD_DIM ** -0.5)
            outs[which][head0 + hh] = sl.astype(BF16)

    pab = _dot(hb, wab_ref[...]) + bab_ref[...]
    par = par_ref[...]
    z = pab + par[1:2, :]
    softplus = jnp.maximum(z, 0.0) + jnp.log1p(jnp.exp(-jnp.abs(z)))
    g = -jnp.exp(par[0:1, :]) * softplus
    beta = jax.nn.sigmoid(pab)

    g3 = jnp.concatenate(_split3(g), axis=1)
    r = lax.broadcasted_iota(jnp.int32, (GROUP, GROUP), 0)
    c = lax.broadcasted_iota(jnp.int32, (GROUP, GROUP), 1)
    same = (r // CHUNK) == (c // CHUNK)
    tri_f = (same & (r >= c)).astype(BF16)
    tri_b = (same & (r <= c)).astype(BF16)
    lane = lax.broadcasted_iota(jnp.int32, (GROUP, LANES), 1)
    for blk in range(tile // GROUP):
        rows = slice(blk * GROUP, (blk + 1) * GROUP)
        pf = _dot(tri_f, g3[rows])
        pb = _dot(tri_b, g3[rows])
        pf = pf[:, 0:LANES] + pf[:, LANES:2 * LANES] + pf[:, 2 * LANES:3 * LANES]
        pb = pb[:, 0:LANES] + pb[:, LANES:2 * LANES] + pb[:, 2 * LANES:3 * LANES]
        gtot = pf + pb - g[rows]
        gc = jnp.where(lane < N_HEADS, pf, pb)
        gb_ref[rows, :] = jnp.where(
            lane < N_AB, gc,
            jnp.where(lane < 2 * N_AB, beta[rows],
                      jnp.where(lane < 3 * N_AB, pltpu.roll(gtot, 2 * N_AB, 1), 0.0)))


def _projection(x, mod3, mod_row, wqkv, bqkv, wab, bab, cw, par, *, tile, row_len):
    b, length, _ = x.shape
    const = lambda shape: pl.BlockSpec(shape, lambda bi, i: (0,) * len(shape),
                                       pipeline_mode=pl.Buffered(1))
    if mod_row is None:
        mod_spec = pl.BlockSpec((None, 3, D_MODEL), lambda bi, i: (bi, 0, 0))
    else:
        mod_spec = pl.BlockSpec((None, 3, D_MODEL), lambda bi, i: (mod_row, 0, 0))
    head_spec = pl.BlockSpec((None, N_HEADS, tile, HEAD_DIM), lambda bi, i: (bi, 0, i, 0))
    head_shape = jax.ShapeDtypeStruct((b, N_HEADS, length, HEAD_DIM), BF16)
    return pl.pallas_call(
        functools.partial(_proj_kernel, tile=tile, row_len=row_len),
        grid=(b, length // tile),
        in_specs=[
            pl.BlockSpec((None, tile, D_MODEL), lambda bi, i: (bi, i, 0)),
            mod_spec,
            const((D_MODEL, 3 * WIDTH_A)),
            const((1, 3 * WIDTH_A)),
            const((D_MODEL, LANES)),
            const((1, LANES)),
            const((3, 3 * WIDTH_A)),
            const((2, LANES)),
        ],
        out_specs=[head_spec, head_spec, head_spec,
                   pl.BlockSpec((None, tile, LANES), lambda bi, i: (bi, i, 0))],
        out_shape=[head_shape, head_shape, head_shape,
                   jax.ShapeDtypeStruct((b, length, LANES), F32)],
        compiler_params=pltpu.CompilerParams(
            dimension_semantics=("arbitrary", "arbitrary"),
            vmem_limit_bytes=48 * MIB),
        name="proj",
    )(x, mod3, wqkv, bqkv, wab, bab, cw, par)


def _gdn_prep(q_ref, k_ref, v_ref, gb_ref, gr_ref, mask_ref, w2q_ref, r_ref,
              row0, grp, ch0, dirn, need_o):
    rows = pl.ds(row0, GROUP)
    q = q_ref[rows, :]
    k = k_ref[rows, :]
    v = v_ref[rows, :]
    gcol = gb_ref[rows, dirn:dirn + 1]
    bcol = gb_ref[rows, 2 + dirn:3 + dirn]
    tcol = gb_ref[rows, 4 + dirn:5 + dirn]
    grow = gr_ref[grp, dirn:dirn + 1, :]
    ecol = jnp.exp(gcol)
    kdcol = jnp.exp(tcol - gcol)
    kf = k.astype(F32)
    kb = kf * bcol

    lhs = jnp.concatenate([q, kb.astype(BF16)], axis=0)
    a = lax.dot_general(lhs, k, (((1,), (1,)), ((), ())), preferred_element_type=F32)
    decay = jnp.exp((gcol - grow) + mask_ref[dirn])
    qk = a[0:GROUP] * decay
    mf = a[GROUP:2 * GROUP] * decay

    n1 = mf * mask_ref[2]
    n1b = n1.astype(BF16)
    n2b = _dot(n1b, n1b).astype(BF16)
    t = mask_ref[5] + n1
    t = t + _dot(t.astype(BF16), n2b)
    n4b = _dot(n2b, n2b).astype(BF16)
    t = t + _dot(t.astype(BF16), n4b)
    n8b = _dot(n4b, n4b).astype(BF16)
    t = t + _dot(t.astype(BF16), n8b)
    for lvl in (3, 4):
        tb = t.astype(BF16)
        xm = _dot((mf * mask_ref[lvl]).astype(BF16), tb).astype(BF16)
        t = t - _dot(tb, xm)
    tb = t.astype(BF16)

    rhs = jnp.concatenate([(kb * ecol).astype(BF16), (v.astype(F32) * bcol).astype(BF16)], axis=1)
    wu = _dot(tb, rhs).astype(BF16)
    kdec = (kf * kdcol).astype(BF16)
    for ci in range(CHUNKS_PER_GROUP):
        rs = slice(ci * CHUNK, (ci + 1) * CHUNK)
        z = lax.dot_general(kdec[rs], wu[rs], (((0,), (0,)), ((), ())),
                            preferred_element_type=F32)
        w2q_ref[dirn, ch0 + ci, 0:HEAD_DIM, :] = z[:, 0:HEAD_DIM].astype(BF16)
        r_ref[dirn, ch0 + ci] = z[:, HEAD_DIM:2 * HEAD_DIM]
    if not need_o:
        return None
    x2 = _dot(qk.astype(BF16), wu)
    qe = (q.astype(F32) * ecol - x2[:, 0:HEAD_DIM]).astype(BF16)
    for ci in range(CHUNKS_PER_GROUP):
        rs = slice(ci * CHUNK, (ci + 1) * CHUNK)
        w2q_ref[dirn, ch0 + ci, HEAD_DIM:HEAD_DIM + CHUNK, :] = qe[rs]
    return x2[:, HEAD_DIM:2 * HEAD_DIM]


def _gdn_kernel(qc_ref, kc_ref, vc_ref, gbc_ref, grc_ref,
                qx_ref, kx_ref, vx_ref, gbx_ref, grx_ref,
                o_ref, mask_ref, w2q_ref, r_ref, *, n_ctx_chunks, n_lat_chunks):
    r = lax.broadcasted_iota(jnp.int32, (GROUP, GROUP), 0)
    c = lax.broadcasted_iota(jnp.int32, (GROUP, GROUP), 1)
    same64 = (r // CHUNK) == (c // CHUNK)
    same32 = (r // 32) == (c // 32)
    same16 = (r // 16) == (c // 16)
    neg_inf = -jnp.inf
    mask_ref[0] = jnp.where(same64 & (r >= c), 0.0, neg_inf)
    mask_ref[1] = jnp.where(same64 & (r <= c), 0.0, neg_inf)
    mask_ref[2] = jnp.where(same16 & (r != c), -1.0, 0.0)
    mask_ref[3] = jnp.where(same32 & jnp.logical_not(same16), 1.0, 0.0)
    mask_ref[4] = jnp.where(jnp.logical_not(same32), 1.0, 0.0)
    mask_ref[5] = jnp.where(r == c, 1.0, 0.0)

    for grp in range(n_ctx_chunks // CHUNKS_PER_GROUP):
        for dirn in range(N_DIR):
            _gdn_prep(qc_ref, kc_ref, vc_ref, gbc_ref, grc_ref, mask_ref, w2q_ref, r_ref,
                      grp * GROUP, grp, grp * CHUNKS_PER_GROUP, dirn, False)

    def prep_body(grp, carry):
        row0 = pl.multiple_of(grp * GROUP, GROUP)
        ch0 = n_ctx_chunks + grp * CHUNKS_PER_GROUP
        o0 = [_gdn_prep(qx_ref, kx_ref, vx_ref, gbx_ref, grx_ref, mask_ref, w2q_ref, r_ref,
                        row0, grp, ch0, dirn, True) for dirn in range(N_DIR)]
        o_ref[pl.ds(row0, GROUP), :] = o0[0] + o0[1]
        return carry

    lax.fori_loop(0, n_lat_chunks // CHUNKS_PER_GROUP, prep_body, 0)

    def scan_step(gb_ref, ch_base, n_chunks, need_o, i, states):
        new_states = []
        for dirn in range(N_DIR):
            s = states[dirn]
            ci = i if dirn == 0 else n_chunks - 1 - i
            row = pl.multiple_of(ci * CHUNK, CHUNK)
            d = jnp.exp(gb_ref[pl.ds(row, 1), 4 + dirn:5 + dirn])
            sb = s.astype(BF16)
            if need_o:
                y = _dot(w2q_ref[dirn, ch_base + ci], sb)
                o_ref[pl.ds(row, CHUNK), :] += y[HEAD_DIM:HEAD_DIM + CHUNK]
                y = y[0:HEAD_DIM]
            else:
                y = _dot(w2q_ref[dirn, ch_base + ci, 0:HEAD_DIM, :], sb)
            new_states.append(d * s + r_ref[dirn, ch_base + ci] - y)
        return tuple(new_states)

    zero = jnp.zeros((HEAD_DIM, HEAD_DIM), F32)
    states = lax.fori_loop(
        0, n_ctx_chunks,
        functools.partial(scan_step, gbc_ref, 0, n_ctx_chunks, False), (zero, zero))
    lax.fori_loop(
        0, n_lat_chunks,
        functools.partial(scan_step, gbx_ref, n_ctx_chunks, n_lat_chunks, True), states)


def _gdn(qc, kc, vc, gbc, grc, qx, kx, vx, gbx, grx):
    b, nh, lc, _ = qc.shape
    lx = qx.shape[2]
    n_ctx_chunks = lc // CHUNK
    n_lat_chunks = lx // CHUNK
    n_chunks = n_ctx_chunks + n_lat_chunks

    def seq_spec(length):
        return pl.BlockSpec((None, None, length, HEAD_DIM), lambda bi, hi: (bi, hi, 0, 0))

    def gb_spec(length):
        return pl.BlockSpec((None, None, length, 6), lambda bi, hi: (bi, hi, 0, 0))

    def gr_spec(length):
        return pl.BlockSpec((None, None, length // GROUP, 4, GROUP),
                            lambda bi, hi: (bi, hi, 0, 0, 0))

    return pl.pallas_call(
        functools.partial(_gdn_kernel, n_ctx_chunks=n_ctx_chunks, n_lat_chunks=n_lat_chunks),
        grid=(b, nh),
        in_specs=[seq_spec(lc), seq_spec(lc), seq_spec(lc), gb_spec(lc), gr_spec(lc),
                  seq_spec(lx), seq_spec(lx), seq_spec(lx), gb_spec(lx), gr_spec(lx)],
        out_specs=seq_spec(lx),
        out_shape=jax.ShapeDtypeStruct((b, nh, lx, HEAD_DIM), F32),
        scratch_shapes=[
            pltpu.VMEM((6, GROUP, GROUP), F32),
            pltpu.VMEM((N_DIR, n_chunks, HEAD_DIM + CHUNK, HEAD_DIM), BF16),
            pltpu.VMEM((N_DIR, n_chunks, HEAD_DIM, HEAD_DIM), F32),
        ],
        compiler_params=pltpu.CompilerParams(
            dimension_semantics=("arbitrary", "arbitrary"),
            vmem_limit_bytes=52 * MIB),
        name="gdn",
    )(qc, kc, vc, gbc, grc, qx, kx, vx, gbx, grx)


def _merge_kernel(x_ref, mod_ref, o_ref, wza_ref, wxin_ref, wbg_ref, wcg_ref, wzb_ref,
                  wga_ref, wgb_ref, wa_ref, wb_ref, wout_ref, bias_ref, onw_ref,
                  cbw_ref, cbb_ref, lng_ref, lnb_ref, out_ref):
    x = x_ref[...]
    m = mod_ref[...]
    h = _layernorm(x) * (1.0 + m[1:2, :]) + m[0:1, :]
    hb = h.astype(BF16)
    bias = bias_ref[...]

    sz = _silu(_dot(hb, wza_ref[...]) + bias[0:1, :])
    onw = onw_ref[...]
    pieces = []
    for hd in range(N_HEADS):
        o = o_ref[hd]
        o = o * lax.rsqrt(jnp.mean(o * o, axis=-1, keepdims=True) + RMS_EPS) * onw
        pieces.append((o * sz[:, hd * HEAD_DIM:(hd + 1) * HEAD_DIM]).astype(BF16))
    ya = jnp.concatenate(pieces, axis=1)
    merged = jax.nn.sigmoid(_dot(hb, wga_ref[...]) + bias[5:6, :]) * _dot(ya, wa_ref[...])

    acc = jnp.zeros((x.shape[0], D_MODEL), F32)
    for j in range(WIDTH_B // MERGE_COLS):
        cols = slice(j * MERGE_COLS, (j + 1) * MERGE_COLS)
        xin = _dot(hb, wxin_ref[:, cols]) + bias[1:2, cols]
        bg = _dot(hb, wbg_ref[:, cols]) + bias[2:3, cols]
        cg = _dot(hb, wcg_ref[:, cols]) + bias[3:4, cols]
        zb = _dot(hb, wzb_ref[:, cols]) + bias[4:5, cols]
        conv = _row_conv3(cg * xin, cbw_ref[:, cols], GRID_W) + cbb_ref[:, cols]
        yb = bg * conv * _silu(zb)
        acc = acc + _dot(yb.astype(BF16), wb_ref[cols, :])
    merged = merged + jax.nn.sigmoid(_dot(hb, wgb_ref[...]) + bias[6:7, :]) * acc

    y = _dot(merged.astype(BF16), wout_ref[...])
    res = DN_ALPHA * x + m[2:3, :] * y
    out_ref[...] = _layernorm(res) * lng_ref[...] + lnb_ref[...]


def _merge(x, mod3, o, weights, bias7, onw, cbw, cbb, lng, lnb):
    b, length, _ = x.shape
    tile = MERGE_TILE
    const = lambda shape: pl.BlockSpec(shape, lambda bi, i: (0,) * len(shape),
                                       pipeline_mode=pl.Buffered(1))
    sq = const((D_MODEL, D_MODEL))
    return pl.pallas_call(
        _merge_kernel,
        grid=(b, length // tile),
        in_specs=[
            pl.BlockSpec((None, tile, D_MODEL), lambda bi, i: (bi, i, 0)),
            pl.BlockSpec((None, 3, D_MODEL), lambda bi, i: (bi, 0, 0)),
            pl.BlockSpec((None, N_HEADS, tile, HEAD_DIM), lambda bi, i: (bi, 0, i, 0)),
        ] + [sq] * 10 + [
            const((7, D_MODEL)),
            const((1, HEAD_DIM)),
            const((3, WIDTH_B)),
            const((1, WIDTH_B)),
            const((1, D_MODEL)),
            const((1, D_MODEL)),
        ],
        out_specs=pl.BlockSpec((None, tile, D_MODEL), lambda bi, i: (bi, i, 0)),
        out_shape=jax.ShapeDtypeStruct((b, length, D_MODEL), F32),
        compiler_params=pltpu.CompilerParams(
            dimension_semantics=("arbitrary", "arbitrary"),
            vmem_limit_bytes=56 * MIB),
        name="merge",
    )(x, mod3, o, *weights, bias7, onw, cbw, cbb, lng, lnb)


def _decay_layouts(gb, length):
    b = gb.shape[0]
    t = gb[:, :, :3 * N_AB].reshape(b, length, 3, N_DIR, N_HEADS)
    col = jnp.transpose(t, (0, 4, 1, 2, 3)).reshape(b, N_HEADS, length, 3 * N_DIR)
    row = jnp.transpose(t[:, :, 0::2], (0, 4, 2, 3, 1)).reshape(
        b, N_HEADS, 2 * N_DIR, length // GROUP, GROUP)
    row = jnp.transpose(row, (0, 1, 3, 2, 4))
    return col, row


def kernel(x, c, ctx, c_ctx, w_mod, b_mod, w_in, b_in, conv_qkv_w, a_log, dt_bias, o_norm_w,
           conv_b_w, conv_b_b, w_a, w_b, w_out, ln_g, ln_b):
    assert w_mod.shape[0] == DEPTH, "single-layer block only"
    batch, seq, _ = x.shape
    ctx_len = ctx.shape[1]
    assert seq % PROJ_TILE == 0 and seq % MERGE_TILE == 0 and ctx_len % GROUP == 0

    c16 = jnp.zeros((16, D_MODEL), F32).at[:batch].set(c).at[batch].set(c_ctx)
    mod = _modulation(c16, w_mod[0], b_mod[0][None, :])
    mod3 = mod.reshape(16, 3, D_MODEL)

    w = w_in[0]
    bi = b_in[0]
    o_qkv, o_za, o_ab = 0, 3 * WIDTH_A, 4 * WIDTH_A
    o_rest = o_ab + 2 * N_AB
    seg = lambda arr, k: arr[..., o_rest + k * 1024:o_rest + (k + 1) * 1024]
    wqkv = w[:, o_qkv:o_za].astype(BF16)
    bqkv = bi[None, o_qkv:o_za]
    wab = jnp.zeros((D_MODEL, LANES), F32).at[:, :2 * N_AB].set(w[:, o_ab:o_rest]).astype(BF16)
    bab = jnp.zeros((1, LANES), F32).at[0, :2 * N_AB].set(bi[o_ab:o_rest])
    par = jnp.zeros((2, LANES), F32)
    par = par.at[0, :N_AB].set(a_log[0].reshape(-1)).at[1, :N_AB].set(dt_bias[0].reshape(-1))
    cw = conv_qkv_w[0]

    qx, kx, vx, gbx = _projection(x, mod3, None, wqkv, bqkv, wab, bab, cw, par,
                                  tile=PROJ_TILE, row_len=GRID_W)
    qc, kc, vc, gbc = _projection(ctx, mod3, batch, wqkv, bqkv, wab, bab, cw, par,
                                  tile=ctx_len, row_len=ctx_len)
    gbx_col, gbx_row = _decay_layouts(gbx, seq)
    gbc_col, gbc_row = _decay_layouts(gbc, ctx_len)

    o = _gdn(qc, kc, vc, gbc_col, gbc_row, qx, kx, vx, gbx_col, gbx_row)

    w_za = w[:, o_za:o_ab].astype(BF16)
    rest_w = [seg(w, k).astype(BF16) for k in range(6)]
    weights = [w_za, rest_w[0], rest_w[1], rest_w[2], rest_w[3], rest_w[4], rest_w[5],
               w_a[0].astype(BF16), w_b[0].astype(BF16), w_out[0].astype(BF16)]
    bias7 = jnp.stack([bi[o_za:o_ab]] + [seg(bi, k) for k in range(6)], axis=0)
    return _merge(x, mod3, o, weights, bias7, o_norm_w[0][None, :], conv_b_w[0],
                  conv_b_b[0][None, :], ln_g[0][None, :], ln_b[0][None, :])
```

```python
import functools
from typing import NamedTuple

import jax
import jax.numpy as jnp
from jax import lax
from jax.experimental import pallas as pl
from jax.experimental.pallas import tpu as pltpu

D_MODEL = 1024
N_HEADS = 8
HEAD_DIM = 128
WIDTH_A = N_HEADS * HEAD_DIM
WIDTH_B = 1024
N_DIR = 2
CHUNK = 64
GRID_W = 64
DEPTH = 1
DN_ALPHA = (2.0 * DEPTH) ** 0.25
LN_EPS = 1e-5
RMS_EPS = 1e-6
L2_EPS = 1e-6

GROUP = 256
CHUNKS_PER_GROUP = GROUP // CHUNK
LANES = 128
N_AB = N_DIR * N_HEADS

PROJ_TILE = 512
PROJ_COLS = 512
MERGE_TILE = 256
MERGE_COLS = 256
MOD_COLS = 512
PREP_GROUPS = 2
MIB = 1024 * 1024

F32 = jnp.float32
BF16 = jnp.bfloat16


def _dot(a, b):
    return jnp.dot(a, b, preferred_element_type=F32)


def _split3(a):
    hi = a.astype(BF16)
    r = a - hi.astype(F32)
    mid = r.astype(BF16)
    lo = (r - mid.astype(F32)).astype(BF16)
    return hi, mid, lo


def _silu(z):
    return z * jax.nn.sigmoid(z)


def _layernorm(x):
    mu = jnp.mean(x, axis=-1, keepdims=True)
    xc = x - mu
    var = jnp.mean(xc * xc, axis=-1, keepdims=True)
    return xc * lax.rsqrt(var + LN_EPS)


def _row_conv3(u, w, row_len):
    n = u.shape[0]
    pos = lax.broadcasted_iota(jnp.int32, (n, 1), 0) & (row_len - 1)
    has_prev = (pos != 0).astype(F32)
    has_next = (pos != row_len - 1).astype(F32)
    prev = pltpu.roll(u, 1, 0) * has_prev
    nxt = pltpu.roll(u, n - 1, 0) * has_next
    return prev * w[0:1, :] + u * w[1:2, :] + nxt * w[2:3, :]


def _mod_kernel(c_ref, w_ref, b_ref, o_ref):
    s = _silu(c_ref[...])
    s_hi, s_mid, _ = _split3(s)
    w = w_ref[...]
    w_hi, w_mid, _ = _split3(w)
    o_ref[...] = _dot(s_hi, w_hi) + _dot(s_hi, w_mid) + _dot(s_mid, w_hi) + b_ref[...]


def _modulation(c16, w_mod, b_mod):
    n_out = w_mod.shape[1]
    return pl.pallas_call(
        _mod_kernel,
        grid=(n_out // MOD_COLS,),
        in_specs=[
            pl.BlockSpec((16, D_MODEL), lambda j: (0, 0)),
            pl.BlockSpec((D_MODEL, MOD_COLS), lambda j: (0, j)),
            pl.BlockSpec((1, MOD_COLS), lambda j: (0, j)),
        ],
        out_specs=pl.BlockSpec((16, MOD_COLS), lambda j: (0, j)),
        out_shape=jax.ShapeDtypeStruct((16, n_out), F32),
        name="mod",
    )(c16, w_mod, b_mod)


def _proj_kernel(x_ref, mod_ref, wqkv_ref, bqkv_ref, wab_ref, bab_ref, cw_ref, par_ref,
                 q_ref, k_ref, v_ref, gb_ref, *, tile, row_len):
    m = mod_ref[...]
    h = _layernorm(x_ref[...]) * (1.0 + m[1:2, :]) + m[0:1, :]
    hb = h.astype(BF16)

    heads_per_pass = PROJ_COLS // HEAD_DIM
    outs = (q_ref, k_ref, v_ref)
    for j in range(3 * WIDTH_A // PROJ_COLS):
        cols = slice(j * PROJ_COLS, (j + 1) * PROJ_COLS)
        p = _dot(hb, wqkv_ref[:, cols]) + bqkv_ref[:, cols]
        s = _silu(_row_conv3(p, cw_ref[:, cols], row_len))
        which = (j * PROJ_COLS) // WIDTH_A
        head0 = ((j * PROJ_COLS) % WIDTH_A) // HEAD_DIM
        for hh in range(heads_per_pass):
            sl = s[:, hh * HEAD_DIM:(hh + 1) * HEAD_DIM]
            if which < 2:
                sl = sl * lax.rsqrt(jnp.sum(sl * sl, axis=-1, keepdims=True) + L2_EPS)
            if which == 0:
                sl = sl * (HEAD_DIM ** -0.5)
            outs[which][head0 + hh] = sl.astype(BF16)

    pab = _dot(hb, wab_ref[...]) + bab_ref[...]
    par = par_ref[...]
    z = pab + par[1:2, :]
    softplus = jnp.maximum(z, 0.0) + jnp.log1p(jnp.exp(-jnp.abs(z)))
    g = -jnp.exp(par[0:1, :]) * softplus
    beta = jax.nn.sigmoid(pab)

    g3 = jnp.concatenate(_split3(g), axis=1)
    r = lax.broadcasted_iota(jnp.int32, (GROUP, GROUP), 0)
    c = lax.broadcasted_iota(jnp.int32, (GROUP, GROUP), 1)
    same = (r // CHUNK) == (c // CHUNK)
    tri_f = (same & (r >= c)).astype(BF16)
    tri_b = (same & (r <= c)).astype(BF16)
    lane = lax.broadcasted_iota(jnp.int32, (GROUP, LANES), 1)
    for blk in range(tile // GROUP):
        rows = slice(blk * GROUP, (blk + 1) * GROUP)
        pf = _dot(tri_f, g3[rows])
        pb = _dot(tri_b, g3[rows])
        pf = pf[:, 0:LANES] + pf[:, LANES:2 * LANES] + pf[:, 2 * LANES:3 * LANES]
        pb = pb[:, 0:LANES] + pb[:, LANES:2 * LANES] + pb[:, 2 * LANES:3 * LANES]
        gtot = pf + pb - g[rows]
        gc = jnp.where(lane < N_HEADS, pf, pb)
        gb_ref[rows, :] = jnp.where(
            lane < N_AB, gc,
            jnp.where(lane < 2 * N_AB, beta[rows],
                      jnp.where(lane < 3 * N_AB, pltpu.roll(gtot, 2 * N_AB, 1), 0.0)))


def _projection(x, mod3, mod_row, wqkv, bqkv, wab, bab, cw, par, *, tile, row_len):
    b, length, _ = x.shape
    const = lambda shape: pl.BlockSpec(shape, lambda bi, i: (0,) * len(shape),
                                       pipeline_mode=pl.Buffered(1))
    if mod_row is None:
        mod_spec = pl.BlockSpec((None, 3, D_MODEL), lambda bi, i: (bi, 0, 0))
    else:
        mod_spec = pl.BlockSpec((None, 3, D_MODEL), lambda bi, i: (mod_row, 0, 0))
    head_spec = pl.BlockSpec((None, N_HEADS, tile, HEAD_DIM), lambda bi, i: (bi, 0, i, 0))
    head_shape = jax.ShapeDtypeStruct((b, N_HEADS, length, HEAD_DIM), BF16)
    return pl.pallas_call(
        functools.partial(_proj_kernel, tile=tile, row_len=row_len),
        grid=(b, length // tile),
        in_specs=[
            pl.BlockSpec((None, tile, D_MODEL), lambda bi, i: (bi, i, 0)),
            mod_spec,
            const((D_MODEL, 3 * WIDTH_A)),
            const((1, 3 * WIDTH_A)),
            const((D_MODEL, LANES)),
            const((1, LANES)),
            const((3, 3 * WIDTH_A)),
            const((2, LANES)),
        ],
        out_specs=[head_spec, head_spec, head_spec,
                   pl.BlockSpec((None, tile, LANES), lambda bi, i: (bi, i, 0))],
        out_shape=[head_shape, head_shape, head_shape,
                   jax.ShapeDtypeStruct((b, length, LANES), F32)],
        compiler_params=pltpu.CompilerParams(
            dimension_semantics=("arbitrary", "arbitrary"),
            vmem_limit_bytes=48 * MIB),
        name="proj",
    )(x, mod3, wqkv, bqkv, wab, bab, cw, par)


class _Prob(NamedTuple):
    q_ref: object
    k_ref: object
    v_ref: object
    gb_ref: object
    gr_ref: object
    row0: object
    grp: object
    ch0: object
    dirn: int


def _gdn_prep(probs, mask_ref, w2q_ref, r_ref, need_o):
    n = range(len(probs))
    q, k, kf, kb, v, gcol, bcol, ecol, kdcol, a = ([None] * len(probs) for _ in range(10))
    for i, p in enumerate(probs):
        rows = pl.ds(p.row0, GROUP)
        q[i] = p.q_ref[rows, :]
        k[i] = p.k_ref[rows, :]
        v[i] = p.v_ref[rows, :]
        gcol[i] = p.gb_ref[rows, p.dirn:p.dirn + 1]
        bcol[i] = p.gb_ref[rows, 2 + p.dirn:3 + p.dirn]
        tcol = p.gb_ref[rows, 4 + p.dirn:5 + p.dirn]
        ecol[i] = jnp.exp(gcol[i])
        kdcol[i] = jnp.exp(tcol - gcol[i])
        kf[i] = k[i].astype(F32)
        kb[i] = kf[i] * bcol[i]
        lhs = jnp.concatenate([q[i], kb[i].astype(BF16)], axis=0)
        a[i] = lax.dot_general(lhs, k[i], (((1,), (1,)), ((), ())), preferred_element_type=F32)

    qk, mf, t, nb = ([None] * len(probs) for _ in range(4))
    for i, p in enumerate(probs):
        grow = p.gr_ref[p.grp, p.dirn:p.dirn + 1, :]
        decay = jnp.exp((gcol[i] - grow) + mask_ref[p.dirn])
        qk[i] = (a[i][0:GROUP] * decay).astype(BF16)
        mf[i] = a[i][GROUP:2 * GROUP] * decay
        n1 = mf[i] * mask_ref[2]
        nb[i] = n1.astype(BF16)
        t[i] = mask_ref[5] + n1

    for _ in range(3):
        for i in n:
            nb[i] = _dot(nb[i], nb[i]).astype(BF16)
        for i in n:
            t[i] = t[i] + _dot(t[i].astype(BF16), nb[i])
    for lvl in (3, 4):
        tb = [t[i].astype(BF16) for i in n]
        xm = [_dot((mf[i] * mask_ref[lvl]).astype(BF16), tb[i]).astype(BF16) for i in n]
        for i in n:
            t[i] = t[i] - _dot(tb[i], xm[i])

    wu = [None] * len(probs)
    for i in n:
        rhs = jnp.concatenate(
            [(kb[i] * ecol[i]).astype(BF16), (v[i].astype(F32) * bcol[i]).astype(BF16)], axis=1)
        wu[i] = _dot(t[i].astype(BF16), rhs).astype(BF16)
    x2 = [_dot(qk[i], wu[i]) for i in n] if need_o else None
    for i, p in enumerate(probs):
        kdec = (kf[i] * kdcol[i]).astype(BF16)
        for ci in range(CHUNKS_PER_GROUP):
            rs = slice(ci * CHUNK, (ci + 1) * CHUNK)
            z = lax.dot_general(kdec[rs], wu[i][rs], (((0,), (0,)), ((), ())),
                                preferred_element_type=F32)
            w2q_ref[p.dirn, p.ch0 + ci, 0:HEAD_DIM, :] = z[:, 0:HEAD_DIM].astype(BF16)
            r_ref[p.dirn, p.ch0 + ci] = z[:, HEAD_DIM:2 * HEAD_DIM]
    if not need_o:
        return None
    for i, p in enumerate(probs):
        qe = (q[i].astype(F32) * ecol[i] - x2[i][:, 0:HEAD_DIM]).astype(BF16)
        for ci in range(CHUNKS_PER_GROUP):
            rs = slice(ci * CHUNK, (ci + 1) * CHUNK)
            w2q_ref[p.dirn, p.ch0 + ci, HEAD_DIM:HEAD_DIM + CHUNK, :] = qe[rs]
    return [x2[i][:, HEAD_DIM:2 * HEAD_DIM] for i in n]


def _gdn_kernel(qc_ref, kc_ref, vc_ref, gbc_ref, grc_ref,
                qx_ref, kx_ref, vx_ref, gbx_ref, grx_ref,
                o_ref, mask_ref, w2q_ref, r_ref, *, n_ctx_chunks, n_lat_chunks):
    r = lax.broadcasted_iota(jnp.int32, (GROUP, GROUP), 0)
    c = lax.broadcasted_iota(jnp.int32, (GROUP, GROUP), 1)
    same64 = (r // CHUNK) == (c // CHUNK)
    same32 = (r // 32) == (c // 32)
    same16 = (r // 16) == (c // 16)
    neg_inf = -jnp.inf
    mask_ref[0] = jnp.where(same64 & (r >= c), 0.0, neg_inf)
    mask_ref[1] = jnp.where(same64 & (r <= c), 0.0, neg_inf)
    mask_ref[2] = jnp.where(same16 & (r != c), -1.0, 0.0)
    mask_ref[3] = jnp.where(same32 & jnp.logical_not(same16), 1.0, 0.0)
    mask_ref[4] = jnp.where(jnp.logical_not(same32), 1.0, 0.0)
    mask_ref[5] = jnp.where(r == c, 1.0, 0.0)

    for grp in range(n_ctx_chunks // CHUNKS_PER_GROUP):
        _gdn_prep([_Prob(qc_ref, kc_ref, vc_ref, gbc_ref, grc_ref, grp * GROUP, grp,
                         grp * CHUNKS_PER_GROUP, dirn) for dirn in range(N_DIR)],
                  mask_ref, w2q_ref, r_ref, False)

    def prep_body(it, carry):
        probs = []
        for sub in range(PREP_GROUPS):
            grp = it * PREP_GROUPS + sub
            row0 = pl.multiple_of(grp * GROUP, GROUP)
            ch0 = n_ctx_chunks + grp * CHUNKS_PER_GROUP
            probs += [_Prob(qx_ref, kx_ref, vx_ref, gbx_ref, grx_ref, row0, grp, ch0, dirn)
                      for dirn in range(N_DIR)]
        o0 = _gdn_prep(probs, mask_ref, w2q_ref, r_ref, True)
        for sub in range(PREP_GROUPS):
            o_ref[pl.ds(probs[N_DIR * sub].row0, GROUP), :] = o0[N_DIR * sub] + o0[N_DIR * sub + 1]
        return carry

    lax.fori_loop(0, n_lat_chunks // (CHUNKS_PER_GROUP * PREP_GROUPS), prep_body, 0)

    def scan_step(gb_ref, ch_base, n_chunks, need_o, i, states):
        new_states = []
        for dirn in range(N_DIR):
            s = states[dirn]
            ci = i if dirn == 0 else n_chunks - 1 - i
            row = pl.multiple_of(ci * CHUNK, CHUNK)
            d = jnp.exp(gb_ref[pl.ds(row, 1), 4 + dirn:5 + dirn])
            sb = s.astype(BF16)
            if need_o:
                y = _dot(w2q_ref[dirn, ch_base + ci], sb)
                o_ref[pl.ds(row, CHUNK), :] += y[HEAD_DIM:HEAD_DIM + CHUNK]
                y = y[0:HEAD_DIM]
            else:
                y = _dot(w2q_ref[dirn, ch_base + ci, 0:HEAD_DIM, :], sb)
            new_states.append(d * s + r_ref[dirn, ch_base + ci] - y)
        return tuple(new_states)

    zero = jnp.zeros((HEAD_DIM, HEAD_DIM), F32)
    states = lax.fori_loop(
        0, n_ctx_chunks,
        functools.partial(scan_step, gbc_ref, 0, n_ctx_chunks, False), (zero, zero))
    lax.fori_loop(
        0, n_lat_chunks,
        functools.partial(scan_step, gbx_ref, n_ctx_chunks, n_lat_chunks, True), states)


def _gdn(qc, kc, vc, gbc, grc, qx, kx, vx, gbx, grx):
    b, nh, lc, _ = qc.shape
    lx = qx.shape[2]
    n_ctx_chunks = lc // CHUNK
    n_lat_chunks = lx // CHUNK
    n_chunks = n_ctx_chunks + n_lat_chunks

    def seq_spec(length):
        return pl.BlockSpec((None, None, length, HEAD_DIM), lambda bi, hi: (bi, hi, 0, 0))

    def gb_spec(length):
        return pl.BlockSpec((None, None, length, 6), lambda bi, hi: (bi, hi, 0, 0))

    def gr_spec(length):
        return pl.BlockSpec((None, None, length // GROUP, 4, GROUP),
                            lambda bi, hi: (bi, hi, 0, 0, 0))

    return pl.pallas_call(
        functools.partial(_gdn_kernel, n_ctx_chunks=n_ctx_chunks, n_lat_chunks=n_lat_chunks),
        grid=(b, nh),
        in_specs=[seq_spec(lc), seq_spec(lc), seq_spec(lc), gb_spec(lc), gr_spec(lc),
                  seq_spec(lx), seq_spec(lx), seq_spec(lx), gb_spec(lx), gr_spec(lx)],
        out_specs=seq_spec(lx),
        out_shape=jax.ShapeDtypeStruct((b, nh, lx, HEAD_DIM), F32),
        scratch_shapes=[
            pltpu.VMEM((6, GROUP, GROUP), F32),
            pltpu.VMEM((N_DIR, n_chunks, HEAD_DIM + CHUNK, HEAD_DIM), BF16),
            pltpu.VMEM((N_DIR, n_chunks, HEAD_DIM, HEAD_DIM), F32),
        ],
        compiler_params=pltpu.CompilerParams(
            dimension_semantics=("arbitrary", "arbitrary"),
            vmem_limit_bytes=52 * MIB),
        name="gdn",
    )(qc, kc, vc, gbc, grc, qx, kx, vx, gbx, grx)


def _merge_kernel(x_ref, mod_ref, o_ref, wza_ref, wxin_ref, wbg_ref, wcg_ref, wzb_ref,
                  wga_ref, wgb_ref, wa_ref, wb_ref, wout_ref, bias_ref, onw_ref,
                  cbw_ref, cbb_ref, lng_ref, lnb_ref, out_ref):
    x = x_ref[...]
    m = mod_ref[...]
    h = _layernorm(x) * (1.0 + m[1:2, :]) + m[0:1, :]
    hb = h.astype(BF16)
    bias = bias_ref[...]

    sz = _silu(_dot(hb, wza_ref[...]) + bias[0:1, :])
    onw = onw_ref[...]
    pieces = []
    for hd in range(N_HEADS):
        o = o_ref[hd]
        o = o * lax.rsqrt(jnp.mean(o * o, axis=-1, keepdims=True) + RMS_EPS) * onw
        pieces.append((o * sz[:, hd * HEAD_DIM:(hd + 1) * HEAD_DIM]).astype(BF16))
    ya = jnp.concatenate(pieces, axis=1)
    merged = jax.nn.sigmoid(_dot(hb, wga_ref[...]) + bias[5:6, :]) * _dot(ya, wa_ref[...])

    acc = jnp.zeros((x.shape[0], D_MODEL), F32)
    for j in range(WIDTH_B // MERGE_COLS):
        cols = slice(j * MERGE_COLS, (j + 1) * MERGE_COLS)
        xin = _dot(hb, wxin_ref[:, cols]) + bias[1:2, cols]
        bg = _dot(hb, wbg_ref[:, cols]) + bias[2:3, cols]
        cg = _dot(hb, wcg_ref[:, cols]) + bias[3:4, cols]
        zb = _dot(hb, wzb_ref[:, cols]) + bias[4:5, cols]
        conv = _row_conv3(cg * xin, cbw_ref[:, cols], GRID_W) + cbb_ref[:, cols]
        yb = bg * conv * _silu(zb)
        acc = acc + _dot(yb.astype(BF16), wb_ref[cols, :])
    merged = merged + jax.nn.sigmoid(_dot(hb, wgb_ref[...]) + bias[6:7, :]) * acc

    y = _dot(merged.astype(BF16), wout_ref[...])
    res = DN_ALPHA * x + m[2:3, :] * y
    out_ref[...] = _layernorm(res) * lng_ref[...] + lnb_ref[...]


def _merge(x, mod3, o, weights, bias7, onw, cbw, cbb, lng, lnb):
    b, length, _ = x.shape
    tile = MERGE_TILE
    const = lambda shape: pl.BlockSpec(shape, lambda bi, i: (0,) * len(shape),
                                       pipeline_mode=pl.Buffered(1))
    sq = const((D_MODEL, D_MODEL))
    return pl.pallas_call(
        _merge_kernel,
        grid=(b, length // tile),
        in_specs=[
            pl.BlockSpec((None, tile, D_MODEL), lambda bi, i: (bi, i, 0)),
            pl.BlockSpec((None, 3, D_MODEL), lambda bi, i: (bi, 0, 0)),
            pl.BlockSpec((None, N_HEADS, tile, HEAD_DIM), lambda bi, i: (bi, 0, i, 0)),
        ] + [sq] * 10 + [
            const((7, D_MODEL)),
            const((1, HEAD_DIM)),
            const((3, WIDTH_B)),
            const((1, WIDTH_B)),
            const((1, D_MODEL)),
            const((1, D_MODEL)),
        ],
        out_specs=pl.BlockSpec((None, tile, D_MODEL), lambda bi, i: (bi, i, 0)),
        out_shape=jax.ShapeDtypeStruct((b, length, D_MODEL), F32),
        compiler_params=pltpu.CompilerParams(
            dimension_semantics=("arbitrary", "arbitrary"),
            vmem_limit_bytes=56 * MIB),
        name="merge",
    )(x, mod3, o, *weights, bias7, onw, cbw, cbb, lng, lnb)


def _decay_layouts(gb, length):
    b = gb.shape[0]
    t = gb[:, :, :3 * N_AB].reshape(b, length, 3, N_DIR, N_HEADS)
    col = jnp.transpose(t, (0, 4, 1, 2, 3)).reshape(b, N_HEADS, length, 3 * N_DIR)
    row = jnp.transpose(t[:, :, 0::2], (0, 4, 2, 3, 1)).reshape(
        b, N_HEADS, 2 * N_DIR, length // GROUP, GROUP)
    row = jnp.transpose(row, (0, 1, 3, 2, 4))
    return col, row


def kernel(x, c, ctx, c_ctx, w_mod, b_mod, w_in, b_in, conv_qkv_w, a_log, dt_bias, o_norm_w,
           conv_b_w, conv_b_b, w_a, w_b, w_out, ln_g, ln_b):
    assert w_mod.shape[0] == DEPTH, "single-layer block only"
    batch, seq, _ = x.shape
    ctx_len = ctx.shape[1]
    assert seq % PROJ_TILE == 0 and seq % MERGE_TILE == 0 and ctx_len % GROUP == 0

    c16 = jnp.zeros((16, D_MODEL), F32).at[:batch].set(c).at[batch].set(c_ctx)
    mod = _modulation(c16, w_mod[0], b_mod[0][None, :])
    mod3 = mod.reshape(16, 3, D_MODEL)

    w = w_in[0]
    bi = b_in[0]
    o_qkv, o_za, o_ab = 0, 3 * WIDTH_A, 4 * WIDTH_A
    o_rest = o_ab + 2 * N_AB
    seg = lambda arr, k: arr[..., o_rest + k * 1024:o_rest + (k + 1) * 1024]
    wqkv = w[:, o_qkv:o_za].astype(BF16)
    bqkv = bi[None, o_qkv:o_za]
    wab = jnp.zeros((D_MODEL, LANES), F32).at[:, :2 * N_AB].set(w[:, o_ab:o_rest]).astype(BF16)
    bab = jnp.zeros((1, LANES), F32).at[0, :2 * N_AB].set(bi[o_ab:o_rest])
    par = jnp.zeros((2, LANES), F32)
    par = par.at[0, :N_AB].set(a_log[0].reshape(-1)).at[1, :N_AB].set(dt_bias[0].reshape(-1))
    cw = conv_qkv_w[0]

    qx, kx, vx, gbx = _projection(x, mod3, None, wqkv, bqkv, wab, bab, cw, par,
                                  tile=PROJ_TILE, row_len=GRID_W)
    qc, kc, vc, gbc = _projection(ctx, mod3, batch, wqkv, bqkv, wab, bab, cw, par,
                                  tile=ctx_len, row_len=ctx_len)
    gbx_col, gbx_row = _decay_layouts(gbx, seq)
    gbc_col, gbc_row = _decay_layouts(gbc, ctx_len)

    o = _gdn(qc, kc, vc, gbc_col, gbc_row, qx, kx, vx, gbx_col, gbx_row)

    w_za = w[:, o_za:o_ab].astype(BF16)
    rest_w = [seg(w, k).astype(BF16) for k in range(6)]
    weights = [w_za, rest_w[0], rest_w[1], rest_w[2], rest_w[3], rest_w[4], rest_w[5],
               w_a[0].astype(BF16), w_b[0].astype(BF16), w_out[0].astype(BF16)]
    bias7 = jnp.stack([bi[o_za:o_ab]] + [seg(bi, k) for k in range(6)], axis=0)
    return _merge(x, mod3, o, weights, bias7, o_norm_w[0][None, :], conv_b_w[0],
                  conv_b_b[0][None, :], ln_g[0][None, :], ln_b[0][None, :])
```

```python
import functools
from typing import NamedTuple

import jax
import jax.numpy as jnp
from jax import lax
from jax.experimental import pallas as pl
from jax.experimental.pallas import tpu as pltpu

D_MODEL = 1024
N_HEADS = 8
HEAD_DIM = 128
WIDTH_A = N_HEADS * HEAD_DIM
WIDTH_B = 1024
N_DIR = 2
CHUNK = 64
GRID_W = 64
DEPTH = 1
DN_ALPHA = (2.0 * DEPTH) ** 0.25
LN_EPS = 1e-5
RMS_EPS = 1e-6
L2_EPS = 1e-6

GROUP = 128
CHUNKS_PER_GROUP = GROUP // CHUNK
LANES = 128
N_AB = N_DIR * N_HEADS

PROJ_TILE = 512
PROJ_COLS = 512
MERGE_TILE = 256
MERGE_COLS = 256
MOD_COLS = 512
PREP_GROUPS = 8
MIB = 1024 * 1024

F32 = jnp.float32
BF16 = jnp.bfloat16


def _dot(a, b):
    return jnp.dot(a, b, preferred_element_type=F32)


def _aligned(v, m):
    return v if isinstance(v, int) else pl.multiple_of(v, m)


def _split3(a):
    hi = a.astype(BF16)
    r = a - hi.astype(F32)
    mid = r.astype(BF16)
    lo = (r - mid.astype(F32)).astype(BF16)
    return hi, mid, lo


def _silu(z):
    return z * jax.nn.sigmoid(z)


def _layernorm(x):
    mu = jnp.mean(x, axis=-1, keepdims=True)
    xc = x - mu
    var = jnp.mean(xc * xc, axis=-1, keepdims=True)
    return xc * lax.rsqrt(var + LN_EPS)


def _row_conv3(u, w, row_len):
    n = u.shape[0]
    pos = lax.broadcasted_iota(jnp.int32, (n, 1), 0) & (row_len - 1)
    has_prev = (pos != 0).astype(F32)
    has_next = (pos != row_len - 1).astype(F32)
    prev = pltpu.roll(u, 1, 0) * has_prev
    nxt = pltpu.roll(u, n - 1, 0) * has_next
    return prev * w[0:1, :] + u * w[1:2, :] + nxt * w[2:3, :]


def _mod_kernel(c_ref, w_ref, b_ref, o_ref):
    s = _silu(c_ref[...])
    s_hi, s_mid, _ = _split3(s)
    w = w_ref[...]
    w_hi, w_mid, _ = _split3(w)
    o_ref[...] = _dot(s_hi, w_hi) + _dot(s_hi, w_mid) + _dot(s_mid, w_hi) + b_ref[...]


def _modulation(c16, w_mod, b_mod):
    n_out = w_mod.shape[1]
    return pl.pallas_call(
        _mod_kernel,
        grid=(n_out // MOD_COLS,),
        in_specs=[
            pl.BlockSpec((16, D_MODEL), lambda j: (0, 0)),
            pl.BlockSpec((D_MODEL, MOD_COLS), lambda j: (0, j)),
            pl.BlockSpec((1, MOD_COLS), lambda j: (0, j)),
        ],
        out_specs=pl.BlockSpec((16, MOD_COLS), lambda j: (0, j)),
        out_shape=jax.ShapeDtypeStruct((16, n_out), F32),
        name="mod",
    )(c16, w_mod, b_mod)


def _proj_kernel(x_ref, mod_ref, wqkv_ref, bqkv_ref, wab_ref, bab_ref, cw_ref, par_ref,
                 q_ref, k_ref, v_ref, gb_ref, *, tile, row_len):
    m = mod_ref[...]
    h = _layernorm(x_ref[...]) * (1.0 + m[1:2, :]) + m[0:1, :]
    hb = h.astype(BF16)

    heads_per_pass = PROJ_COLS // HEAD_DIM
    outs = (q_ref, k_ref, v_ref)
    for j in range(3 * WIDTH_A // PROJ_COLS):
        cols = slice(j * PROJ_COLS, (j + 1) * PROJ_COLS)
        p = _dot(hb, wqkv_ref[:, cols]) + bqkv_ref[:, cols]
        s = _silu(_row_conv3(p, cw_ref[:, cols], row_len))
        which = (j * PROJ_COLS) // WIDTH_A
        head0 = ((j * PROJ_COLS) % WIDTH_A) // HEAD_DIM
        for hh in range(heads_per_pass):
            sl = s[:, hh * HEAD_DIM:(hh + 1) * HEAD_DIM]
            if which < 2:
                sl = sl * lax.rsqrt(jnp.sum(sl * sl, axis=-1, keepdims=True) + L2_EPS)
            if which == 0:
                sl = sl * (HEAD_DIM ** -0.5)
            outs[which][head0 + hh] = sl.astype(BF16)

    pab = _dot(hb, wab_ref[...]) + bab_ref[...]
    par = par_ref[...]
    z = pab + par[1:2, :]
    softplus = jnp.maximum(z, 0.0) + jnp.log1p(jnp.exp(-jnp.abs(z)))
    g = -jnp.exp(par[0:1, :]) * softplus
    beta = jax.nn.sigmoid(pab)

    g3 = jnp.concatenate(_split3(g), axis=1)
    r = lax.broadcasted_iota(jnp.int32, (GROUP, GROUP), 0)
    c = lax.broadcasted_iota(jnp.int32, (GROUP, GROUP), 1)
    same = (r // CHUNK) == (c // CHUNK)
    tri_f = (same & (r >= c)).astype(BF16)
    tri_b = (same & (r <= c)).astype(BF16)
    lane = lax.broadcasted_iota(jnp.int32, (GROUP, LANES), 1)
    for blk in range(tile // GROUP):
        rows = slice(blk * GROUP, (blk + 1) * GROUP)
        pf = _dot(tri_f, g3[rows])
        pb = _dot(tri_b, g3[rows])
        pf = pf[:, 0:LANES] + pf[:, LANES:2 * LANES] + pf[:, 2 * LANES:3 * LANES]
        pb = pb[:, 0:LANES] + pb[:, LANES:2 * LANES] + pb[:, 2 * LANES:3 * LANES]
        gtot = pf + pb - g[rows]
        gc = jnp.where(lane < N_HEADS, pf, pb)
        gb_ref[rows, :] = jnp.where(
            lane < N_AB, gc,
            jnp.where(lane < 2 * N_AB, beta[rows],
                      jnp.where(lane < 3 * N_AB, pltpu.roll(gtot, 2 * N_AB, 1), 0.0)))


def _projection(x, mod3, mod_row, wqkv, bqkv, wab, bab, cw, par, *, tile, row_len):
    b, length, _ = x.shape
    const = lambda shape: pl.BlockSpec(shape, lambda bi, i: (0,) * len(shape),
                                       pipeline_mode=pl.Buffered(1))
    if mod_row is None:
        mod_spec = pl.BlockSpec((None, 3, D_MODEL), lambda bi, i: (bi, 0, 0))
    else:
        mod_spec = pl.BlockSpec((None, 3, D_MODEL), lambda bi, i: (mod_row, 0, 0))
    head_spec = pl.BlockSpec((None, N_HEADS, tile, HEAD_DIM), lambda bi, i: (bi, 0, i, 0))
    head_shape = jax.ShapeDtypeStruct((b, N_HEADS, length, HEAD_DIM), BF16)
    return pl.pallas_call(
        functools.partial(_proj_kernel, tile=tile, row_len=row_len),
        grid=(b, length // tile),
        in_specs=[
            pl.BlockSpec((None, tile, D_MODEL), lambda bi, i: (bi, i, 0)),
            mod_spec,
            const((D_MODEL, 3 * WIDTH_A)),
            const((1, 3 * WIDTH_A)),
            const((D_MODEL, LANES)),
            const((1, LANES)),
            const((3, 3 * WIDTH_A)),
            const((2, LANES)),
        ],
        out_specs=[head_spec, head_spec, head_spec,
                   pl.BlockSpec((None, tile, LANES), lambda bi, i: (bi, i, 0))],
        out_shape=[head_shape, head_shape, head_shape,
                   jax.ShapeDtypeStruct((b, length, LANES), F32)],
        compiler_params=pltpu.CompilerParams(
            dimension_semantics=("arbitrary", "arbitrary"),
            vmem_limit_bytes=48 * MIB),
        name="proj",
    )(x, mod3, wqkv, bqkv, wab, bab, cw, par)


class _Prob(NamedTuple):
    q_ref: object
    k_ref: object
    v_ref: object
    gb_ref: object
    gr_ref: object
    row0: object
    grp: object
    ch0: object
    dirn: int
    need_o: bool


def _gdn_prep(probs, mask_ref, w2q_ref, r_ref):
    n = range(len(probs))
    q, k, kf, kb, v, gcol, bcol, ecol, kdcol, a = ([None] * len(probs) for _ in range(10))
    for i, p in enumerate(probs):
        rows = pl.ds(p.row0, GROUP)
        q[i] = p.q_ref[rows, :]
        k[i] = p.k_ref[rows, :]
        v[i] = p.v_ref[rows, :]
        gcol[i] = p.gb_ref[rows, p.dirn:p.dirn + 1]
        bcol[i] = p.gb_ref[rows, 2 + p.dirn:3 + p.dirn]
        tcol = p.gb_ref[rows, 4 + p.dirn:5 + p.dirn]
        ecol[i] = jnp.exp(gcol[i])
        kdcol[i] = jnp.exp(tcol - gcol[i])
        kf[i] = k[i].astype(F32)
        kb[i] = kf[i] * bcol[i]
        lhs = jnp.concatenate([q[i], kb[i].astype(BF16)], axis=0)
        a[i] = lax.dot_general(lhs, k[i], (((1,), (1,)), ((), ())), preferred_element_type=F32)

    qk, mf, t, nb = ([None] * len(probs) for _ in range(4))
    for i, p in enumerate(probs):
        grow = p.gr_ref[p.grp, p.dirn:p.dirn + 1, :]
        decay = jnp.exp((gcol[i] - grow) + mask_ref[p.dirn])
        qk[i] = (a[i][0:GROUP] * decay).astype(BF16)
        mf[i] = a[i][GROUP:2 * GROUP] * decay
        n1 = mf[i] * mask_ref[2]
        nb[i] = n1.astype(BF16)
        t[i] = mask_ref[5] + n1

    for _ in range(3):
        for i in n:
            nb[i] = _dot(nb[i], nb[i]).astype(BF16)
        for i in n:
            t[i] = t[i] + _dot(t[i].astype(BF16), nb[i])
    for lvl in (3, 4):
        tb = [t[i].astype(BF16) for i in n]
        xm = [_dot((mf[i] * mask_ref[lvl]).astype(BF16), tb[i]).astype(BF16) for i in n]
        for i in n:
            t[i] = t[i] - _dot(tb[i], xm[i])

    wu = [None] * len(probs)
    for i in n:
        rhs = jnp.concatenate(
            [(kb[i] * ecol[i]).astype(BF16), (v[i].astype(F32) * bcol[i]).astype(BF16)], axis=1)
        wu[i] = _dot(t[i].astype(BF16), rhs).astype(BF16)
    x2 = [_dot(qk[i], wu[i]) if probs[i].need_o else None for i in n]
    for i, p in enumerate(probs):
        kdec = (kf[i] * kdcol[i]).astype(BF16)
        for ci in range(CHUNKS_PER_GROUP):
            rs = slice(ci * CHUNK, (ci + 1) * CHUNK)
            z = lax.dot_general(kdec[rs], wu[i][rs], (((0,), (0,)), ((), ())),
                                preferred_element_type=F32)
            w2q_ref[p.dirn, p.ch0 + ci, 0:HEAD_DIM, :] = z[:, 0:HEAD_DIM].astype(BF16)
            r_ref[p.dirn, p.ch0 + ci] = z[:, HEAD_DIM:2 * HEAD_DIM]
    o0 = [None] * len(probs)
    for i, p in enumerate(probs):
        if not p.need_o:
            continue
        qe = (q[i].astype(F32) * ecol[i] - x2[i][:, 0:HEAD_DIM]).astype(BF16)
        for ci in range(CHUNKS_PER_GROUP):
            rs = slice(ci * CHUNK, (ci + 1) * CHUNK)
            w2q_ref[p.dirn, p.ch0 + ci, HEAD_DIM:HEAD_DIM + CHUNK, :] = qe[rs]
        o0[i] = x2[i][:, HEAD_DIM:2 * HEAD_DIM]
    return o0


def _gdn_kernel(qc_ref, kc_ref, vc_ref, gbc_ref, grc_ref,
                qx_ref, kx_ref, vx_ref, gbx_ref, grx_ref,
                o_ref, mask_ref, w2q_ref, r_ref, o0_ref, *, n_ctx_chunks, n_lat_chunks):
    r = lax.broadcasted_iota(jnp.int32, (GROUP, GROUP), 0)
    c = lax.broadcasted_iota(jnp.int32, (GROUP, GROUP), 1)
    same64 = (r // CHUNK) == (c // CHUNK)
    same32 = (r // 32) == (c // 32)
    same16 = (r // 16) == (c // 16)
    neg_inf = -jnp.inf
    mask_ref[0] = jnp.where(same64 & (r >= c), 0.0, neg_inf)
    mask_ref[1] = jnp.where(same64 & (r <= c), 0.0, neg_inf)
    mask_ref[2] = jnp.where(same16 & (r != c), -1.0, 0.0)
    mask_ref[3] = jnp.where(same32 & jnp.logical_not(same16), 1.0, 0.0)
    mask_ref[4] = jnp.where(jnp.logical_not(same32), 1.0, 0.0)
    mask_ref[5] = jnp.where(r == c, 1.0, 0.0)

    n_lat_groups = n_lat_chunks // CHUNKS_PER_GROUP
    steps_per_iter = CHUNKS_PER_GROUP * PREP_GROUPS
    n_iter = n_lat_chunks // steps_per_iter

    def lat_probs(it):
        probs = []
        for sub in range(PREP_GROUPS):
            fwd_grp = it * PREP_GROUPS + sub
            for dirn, grp in ((0, fwd_grp), (1, n_lat_groups - 1 - fwd_grp)):
                probs.append(_Prob(qx_ref, kx_ref, vx_ref, gbx_ref, grx_ref,
                                   _aligned(grp * GROUP, GROUP), grp,
                                   n_ctx_chunks + grp * CHUNKS_PER_GROUP, dirn, True))
        return probs

    def prep(probs):
        for p, o0 in zip(probs, _gdn_prep(probs, mask_ref, w2q_ref, r_ref)):
            if p.need_o:
                o0_ref[p.dirn, pl.ds(p.row0, GROUP), :] = o0

    def scan_step(gb_ref, ch_base, n_chunks, need_o, i, states):
        new_states = []
        for dirn in range(N_DIR):
            s = states[dirn]
            ci = i if dirn == 0 else n_chunks - 1 - i
            row = _aligned(ci * CHUNK, CHUNK)
            d = jnp.exp(gb_ref[pl.ds(row, 1), 4 + dirn:5 + dirn])
            sb = s.astype(BF16)
            if need_o:
                y = _dot(w2q_ref[dirn, ch_base + ci], sb)
                o0_ref[dirn, pl.ds(row, CHUNK), :] += y[HEAD_DIM:HEAD_DIM + CHUNK]
                y = y[0:HEAD_DIM]
            else:
                y = _dot(w2q_ref[dirn, ch_base + ci, 0:HEAD_DIM, :], sb)
            new_states.append(d * s + r_ref[dirn, ch_base + ci] - y)
        return tuple(new_states)

    def lat_scan(it, states):
        for j in range(steps_per_iter):
            states = scan_step(gbx_ref, n_ctx_chunks, n_lat_chunks, True,
                               it * steps_per_iter + j, states)
        return states

    ctx_probs = [_Prob(qc_ref, kc_ref, vc_ref, gbc_ref, grc_ref, grp * GROUP, grp,
                       grp * CHUNKS_PER_GROUP, dirn, False)
                 for grp in range(n_ctx_chunks // CHUNKS_PER_GROUP) for dirn in range(N_DIR)]
    prep(ctx_probs + lat_probs(0))
    zero = jnp.zeros((HEAD_DIM, HEAD_DIM), F32)
    states = (zero, zero)
    for i in range(n_ctx_chunks):
        states = scan_step(gbc_ref, 0, n_ctx_chunks, False, i, states)

    def body(it, states):
        states = lat_scan(it - 1, states)
        prep(lat_probs(it))
        return states

    states = lax.fori_loop(1, n_iter, body, states)
    lat_scan(n_iter - 1, states)
    o_ref[...] = (o0_ref[0] + o0_ref[1]).astype(o_ref.dtype)


def _gdn(qc, kc, vc, gbc, grc, qx, kx, vx, gbx, grx):
    b, nh, lc, _ = qc.shape
    lx = qx.shape[2]
    n_ctx_chunks = lc // CHUNK
    n_lat_chunks = lx // CHUNK
    n_chunks = n_ctx_chunks + n_lat_chunks

    def seq_spec(length):
        return pl.BlockSpec((None, None, length, HEAD_DIM), lambda bi, hi: (bi, hi, 0, 0))

    def gb_spec(length):
        return pl.BlockSpec((None, None, length, 6), lambda bi, hi: (bi, hi, 0, 0))

    def gr_spec(length):
        return pl.BlockSpec((None, None, length // GROUP, 4, GROUP),
                            lambda bi, hi: (bi, hi, 0, 0, 0))

    return pl.pallas_call(
        functools.partial(_gdn_kernel, n_ctx_chunks=n_ctx_chunks, n_lat_chunks=n_lat_chunks),
        grid=(b, nh),
        in_specs=[seq_spec(lc), seq_spec(lc), seq_spec(lc), gb_spec(lc), gr_spec(lc),
                  seq_spec(lx), seq_spec(lx), seq_spec(lx), gb_spec(lx), gr_spec(lx)],
        out_specs=seq_spec(lx),
        out_shape=jax.ShapeDtypeStruct((b, nh, lx, HEAD_DIM), BF16),
        scratch_shapes=[
            pltpu.VMEM((6, GROUP, GROUP), F32),
            pltpu.VMEM((N_DIR, n_chunks, HEAD_DIM + CHUNK, HEAD_DIM), BF16),
            pltpu.VMEM((N_DIR, n_chunks, HEAD_DIM, HEAD_DIM), F32),
            pltpu.VMEM((N_DIR, lx, HEAD_DIM), F32),
        ],
        compiler_params=pltpu.CompilerParams(
            dimension_semantics=("arbitrary", "arbitrary"),
            vmem_limit_bytes=52 * MIB),
        name="gdn",
    )(qc, kc, vc, gbc, grc, qx, kx, vx, gbx, grx)


def _merge_kernel(x_ref, mod_ref, o_ref, wza_ref, wxin_ref, wbg_ref, wcg_ref, wzb_ref,
                  wga_ref, wgb_ref, wa_ref, wb_ref, wout_ref, bias_ref, onw_ref,
                  cbw_ref, cbb_ref, lng_ref, lnb_ref, out_ref):
    x = x_ref[...]
    m = mod_ref[...]
    h = _layernorm(x) * (1.0 + m[1:2, :]) + m[0:1, :]
    hb = h.astype(BF16)
    bias = bias_ref[...]

    sz = _silu(_dot(hb, wza_ref[...]) + bias[0:1, :])
    onw = onw_ref[...]
    pieces = []
    for hd in range(N_HEADS):
        o = o_ref[hd].astype(F32)
        o = o * lax.rsqrt(jnp.mean(o * o, axis=-1, keepdims=True) + RMS_EPS) * onw
        pieces.append((o * sz[:, hd * HEAD_DIM:(hd + 1) * HEAD_DIM]).astype(BF16))
    ya = jnp.concatenate(pieces, axis=1)
    merged = jax.nn.sigmoid(_dot(hb, wga_ref[...]) + bias[5:6, :]) * _dot(ya, wa_ref[...])

    acc = jnp.zeros((x.shape[0], D_MODEL), F32)
    for j in range(WIDTH_B // MERGE_COLS):
        cols = slice(j * MERGE_COLS, (j + 1) * MERGE_COLS)
        xin = _dot(hb, wxin_ref[:, cols]) + bias[1:2, cols]
        bg = _dot(hb, wbg_ref[:, cols]) + bias[2:3, cols]
        cg = _dot(hb, wcg_ref[:, cols]) + bias[3:4, cols]
        zb = _dot(hb, wzb_ref[:, cols]) + bias[4:5, cols]
        conv = _row_conv3(cg * xin, cbw_ref[:, cols], GRID_W) + cbb_ref[:, cols]
        yb = bg * conv * _silu(zb)
        acc = acc + _dot(yb.astype(BF16), wb_ref[cols, :])
    merged = merged + jax.nn.sigmoid(_dot(hb, wgb_ref[...]) + bias[6:7, :]) * acc

    y = _dot(merged.astype(BF16), wout_ref[...])
    res = DN_ALPHA * x + m[2:3, :] * y
    out_ref[...] = _layernorm(res) * lng_ref[...] + lnb_ref[...]


def _merge(x, mod3, o, weights, bias7, onw, cbw, cbb, lng, lnb):
    b, length, _ = x.shape
    tile = MERGE_TILE
    const = lambda shape: pl.BlockSpec(shape, lambda bi, i: (0,) * len(shape),
                                       pipeline_mode=pl.Buffered(1))
    sq = const((D_MODEL, D_MODEL))
    return pl.pallas_call(
        _merge_kernel,
        grid=(b, length // tile),
        in_specs=[
            pl.BlockSpec((None, tile, D_MODEL), lambda bi, i: (bi, i, 0)),
            pl.BlockSpec((None, 3, D_MODEL), lambda bi, i: (bi, 0, 0)),
            pl.BlockSpec((None, N_HEADS, tile, HEAD_DIM), lambda bi, i: (bi, 0, i, 0)),
        ] + [sq] * 10 + [
            const((7, D_MODEL)),
            const((1, HEAD_DIM)),
            const((3, WIDTH_B)),
            const((1, WIDTH_B)),
            const((1, D_MODEL)),
            const((1, D_MODEL)),
        ],
        out_specs=pl.BlockSpec((None, tile, D_MODEL), lambda bi, i: (bi, i, 0)),
        out_shape=jax.ShapeDtypeStruct((b, length, D_MODEL), F32),
        compiler_params=pltpu.CompilerParams(
            dimension_semantics=("arbitrary", "arbitrary"),
            vmem_limit_bytes=56 * MIB),
        name="merge",
    )(x, mod3, o, *weights, bias7, onw, cbw, cbb, lng, lnb)


def _decay_layouts(gb, length):
    b = gb.shape[0]
    t = gb[:, :, :3 * N_AB].reshape(b, length, 3, N_DIR, N_HEADS)
    col = jnp.transpose(t, (0, 4, 1, 2, 3)).reshape(b, N_HEADS, length, 3 * N_DIR)
    row = jnp.transpose(t[:, :, 0::2], (0, 4, 2, 3, 1)).reshape(
        b, N_HEADS, 2 * N_DIR, length // GROUP, GROUP)
    row = jnp.transpose(row, (0, 1, 3, 2, 4))
    return col, row


def kernel(x, c, ctx, c_ctx, w_mod, b_mod, w_in, b_in, conv_qkv_w, a_log, dt_bias, o_norm_w,
           conv_b_w, conv_b_b, w_a, w_b, w_out, ln_g, ln_b):
    assert w_mod.shape[0] == DEPTH, "single-layer block only"
    batch, seq, _ = x.shape
    ctx_len = ctx.shape[1]
    assert seq % PROJ_TILE == 0 and seq % MERGE_TILE == 0 and ctx_len % GROUP == 0

    c16 = jnp.zeros((16, D_MODEL), F32).at[:batch].set(c).at[batch].set(c_ctx)
    mod = _modulation(c16, w_mod[0], b_mod[0][None, :])
    mod3 = mod.reshape(16, 3, D_MODEL)

    w = w_in[0]
    bi = b_in[0]
    o_qkv, o_za, o_ab = 0, 3 * WIDTH_A, 4 * WIDTH_A
    o_rest = o_ab + 2 * N_AB
    seg = lambda arr, k: arr[..., o_rest + k * 1024:o_rest + (k + 1) * 1024]
    wqkv = w[:, o_qkv:o_za].astype(BF16)
    bqkv = bi[None, o_qkv:o_za]
    wab = jnp.zeros((D_MODEL, LANES), F32).at[:, :2 * N_AB].set(w[:, o_ab:o_rest]).astype(BF16)
    bab = jnp.zeros((1, LANES), F32).at[0, :2 * N_AB].set(bi[o_ab:o_rest])
    par = jnp.zeros((2, LANES), F32)
    par = par.at[0, :N_AB].set(a_log[0].reshape(-1)).at[1, :N_AB].set(dt_bias[0].reshape(-1))
    cw = conv_qkv_w[0]

    qx, kx, vx, gbx = _projection(x, mod3, None, wqkv, bqkv, wab, bab, cw, par,
                                  tile=PROJ_TILE, row_len=GRID_W)
    qc, kc, vc, gbc = _projection(ctx, mod3, batch, wqkv, bqkv, wab, bab, cw, par,
                                  tile=ctx_len, row_len=ctx_len)
    gbx_col, gbx_row = _decay_layouts(gbx, seq)
    gbc_col, gbc_row = _decay_layouts(gbc, ctx_len)

    o = _gdn(qc, kc, vc, gbc_col, gbc_row, qx, kx, vx, gbx_col, gbx_row)

    w_za = w[:, o_za:o_ab].astype(BF16)
    rest_w = [seg(w, k).astype(BF16) for k in range(6)]
    weights = [w_za, rest_w[0], rest_w[1], rest_w[2], rest_w[3], rest_w[4], rest_w[5],
               w_a[0].astype(BF16), w_b[0].astype(BF16), w_out[0].astype(BF16)]
    bias7 = jnp.stack([bi[o_za:o_ab]] + [seg(bi, k) for k in range(6)], axis=0)
    return _merge(x, mod3, o, weights, bias7, o_norm_w[0][None, :], conv_b_w[0],
                  conv_b_b[0][None, :], ln_g[0][None, :], ln_b[0][None, :])
```

```python
import functools
from typing import NamedTuple

import jax
import jax.numpy as jnp
from jax import lax
from jax.experimental import pallas as pl
from jax.experimental.pallas import tpu as pltpu

D_MODEL = 1024
N_HEADS = 8
HEAD_DIM = 128
WIDTH_A = N_HEADS * HEAD_DIM
WIDTH_B = 1024
N_DIR = 2
CHUNK = 64
GRID_W = 64
DEPTH = 1
DN_ALPHA = (2.0 * DEPTH) ** 0.25
LN_EPS = 1e-5
RMS_EPS = 1e-6
L2_EPS = 1e-6

GROUP = 128
CHUNKS_PER_GROUP = GROUP // CHUNK
LANES = 128
N_AB = N_DIR * N_HEADS

PROJ_TILE = 512
PROJ_COLS = 512
MERGE_TILE = 256
MERGE_COLS = 256
MOD_COLS = 512
PREP_GROUPS = 8
MIB = 1024 * 1024

F32 = jnp.float32
BF16 = jnp.bfloat16


def _dot(a, b):
    return jnp.dot(a, b, preferred_element_type=F32)


def _aligned(v, m):
    return v if isinstance(v, int) else pl.multiple_of(v, m)


def _split3(a):
    hi = a.astype(BF16)
    r = a - hi.astype(F32)
    mid = r.astype(BF16)
    lo = (r - mid.astype(F32)).astype(BF16)
    return hi, mid, lo


def _silu(z):
    return z * jax.nn.sigmoid(z)


def _layernorm(x):
    mu = jnp.mean(x, axis=-1, keepdims=True)
    xc = x - mu
    var = jnp.mean(xc * xc, axis=-1, keepdims=True)
    return xc * lax.rsqrt(var + LN_EPS)


def _row_conv3(u, w, row_len):
    n = u.shape[0]
    pos = lax.broadcasted_iota(jnp.int32, (n, 1), 0) & (row_len - 1)
    has_prev = (pos != 0).astype(F32)
    has_next = (pos != row_len - 1).astype(F32)
    prev = pltpu.roll(u, 1, 0) * has_prev
    nxt = pltpu.roll(u, n - 1, 0) * has_next
    return prev * w[0:1, :] + u * w[1:2, :] + nxt * w[2:3, :]


def _mod_kernel(c_ref, w_ref, b_ref, o_ref):
    s = _silu(c_ref[...])
    s_hi, s_mid, _ = _split3(s)
    w = w_ref[...]
    w_hi, w_mid, _ = _split3(w)
    o_ref[...] = _dot(s_hi, w_hi) + _dot(s_hi, w_mid) + _dot(s_mid, w_hi) + b_ref[...]


def _modulation(c16, w_mod, b_mod):
    n_out = w_mod.shape[1]
    return pl.pallas_call(
        _mod_kernel,
        grid=(n_out // MOD_COLS,),
        in_specs=[
            pl.BlockSpec((16, D_MODEL), lambda j: (0, 0)),
            pl.BlockSpec((D_MODEL, MOD_COLS), lambda j: (0, j)),
            pl.BlockSpec((1, MOD_COLS), lambda j: (0, j)),
        ],
        out_specs=pl.BlockSpec((16, MOD_COLS), lambda j: (0, j)),
        out_shape=jax.ShapeDtypeStruct((16, n_out), F32),
        name="mod",
    )(c16, w_mod, b_mod)


def _proj_kernel(x_ref, mod_ref, wqkv_ref, bqkv_ref, wab_ref, bab_ref, cw_ref, par_ref,
                 q_ref, k_ref, v_ref, gb_ref, *, tile, row_len):
    m = mod_ref[...]
    h = _layernorm(x_ref[...]) * (1.0 + m[1:2, :]) + m[0:1, :]
    hb = h.astype(BF16)

    heads_per_pass = PROJ_COLS // HEAD_DIM
    outs = (q_ref, k_ref, v_ref)
    for j in range(3 * WIDTH_A // PROJ_COLS):
        cols = slice(j * PROJ_COLS, (j + 1) * PROJ_COLS)
        p = _dot(hb, wqkv_ref[:, cols]) + bqkv_ref[:, cols]
        s = _silu(_row_conv3(p, cw_ref[:, cols], row_len))
        which = (j * PROJ_COLS) // WIDTH_A
        head0 = ((j * PROJ_COLS) % WIDTH_A) // HEAD_DIM
        for hh in range(heads_per_pass):
            sl = s[:, hh * HEAD_DIM:(hh + 1) * HEAD_DIM]
            if which < 2:
                sl = sl * lax.rsqrt(jnp.sum(sl * sl, axis=-1, keepdims=True) + L2_EPS)
            if which == 0:
                sl = sl * (HEAD_DIM ** -0.5)
            outs[which][head0 + hh] = sl.astype(BF16)

    pab = _dot(hb, wab_ref[...]) + bab_ref[...]
    par = par_ref[...]
    z = pab + par[1:2, :]
    softplus = jnp.maximum(z, 0.0) + jnp.log1p(jnp.exp(-jnp.abs(z)))
    g = -jnp.exp(par[0:1, :]) * softplus
    beta = jax.nn.sigmoid(pab)

    g3 = jnp.concatenate(_split3(g), axis=1)
    r = lax.broadcasted_iota(jnp.int32, (GROUP, GROUP), 0)
    c = lax.broadcasted_iota(jnp.int32, (GROUP, GROUP), 1)
    same = (r // CHUNK) == (c // CHUNK)
    tri_f = (same & (r >= c)).astype(BF16)
    tri_b = (same & (r <= c)).astype(BF16)
    lane = lax.broadcasted_iota(jnp.int32, (GROUP, LANES), 1)
    for blk in range(tile // GROUP):
        rows = slice(blk * GROUP, (blk + 1) * GROUP)
        pf = _dot(tri_f, g3[rows])
        pb = _dot(tri_b, g3[rows])
        pf = pf[:, 0:LANES] + pf[:, LANES:2 * LANES] + pf[:, 2 * LANES:3 * LANES]
        pb = pb[:, 0:LANES] + pb[:, LANES:2 * LANES] + pb[:, 2 * LANES:3 * LANES]
        gtot = pf + pb - g[rows]
        gc = jnp.where(lane < N_HEADS, pf, pb)
        gb_ref[rows, :] = jnp.where(
            lane < N_AB, gc,
            jnp.where(lane < 2 * N_AB, beta[rows],
                      jnp.where(lane < 3 * N_AB, pltpu.roll(gtot, 2 * N_AB, 1), 0.0)))


def _projection(x, mod3, mod_row, wqkv, bqkv, wab, bab, cw, par, *, tile, row_len):
    b, length, _ = x.shape
    const = lambda shape: pl.BlockSpec(shape, lambda bi, i: (0,) * len(shape),
                                       pipeline_mode=pl.Buffered(1))
    if mod_row is None:
        mod_spec = pl.BlockSpec((None, 3, D_MODEL), lambda bi, i: (bi, 0, 0))
    else:
        mod_spec = pl.BlockSpec((None, 3, D_MODEL), lambda bi, i: (mod_row, 0, 0))
    head_spec = pl.BlockSpec((None, N_HEADS, tile, HEAD_DIM), lambda bi, i: (bi, 0, i, 0))
    head_shape = jax.ShapeDtypeStruct((b, N_HEADS, length, HEAD_DIM), BF16)
    return pl.pallas_call(
        functools.partial(_proj_kernel, tile=tile, row_len=row_len),
        grid=(b, length // tile),
        in_specs=[
            pl.BlockSpec((None, tile, D_MODEL), lambda bi, i: (bi, i, 0)),
            mod_spec,
            const((D_MODEL, 3 * WIDTH_A)),
            const((1, 3 * WIDTH_A)),
            const((D_MODEL, LANES)),
            const((1, LANES)),
            const((3, 3 * WIDTH_A)),
            const((2, LANES)),
        ],
        out_specs=[head_spec, head_spec, head_spec,
                   pl.BlockSpec((None, tile, LANES), lambda bi, i: (bi, i, 0))],
        out_shape=[head_shape, head_shape, head_shape,
                   jax.ShapeDtypeStruct((b, length, LANES), F32)],
        compiler_params=pltpu.CompilerParams(
            dimension_semantics=("arbitrary", "arbitrary"),
            vmem_limit_bytes=48 * MIB),
        name="proj",
    )(x, mod3, wqkv, bqkv, wab, bab, cw, par)


class _Prob(NamedTuple):
    q_ref: object
    k_ref: object
    v_ref: object
    gr_ref: object
    row0: object
    grp: object
    ch0: object
    dirn: int
    need_o: bool


_ROW_GC, _ROW_BETA, _ROW_GTOT = 0, 2, 4


PREP_STAGES = 12


def _gdn_prep(probs, mask_ref, w2q_ref, r_ref, d_ref, o0_ref):
    n = range(len(probs))
    lane = lax.broadcasted_iota(jnp.int32, (1, GROUP), 1)
    chunk_cols = [((lane // CHUNK) == ci).astype(F32) for ci in range(CHUNKS_PER_GROUP)]
    q, kt, kb, v, rows8, gcol, bcol, ecol, a = ([None] * len(probs) for _ in range(9))
    for i, p in enumerate(probs):
        rows = pl.ds(p.row0, GROUP)
        q[i] = p.q_ref[rows, :]
        kf = p.k_ref[rows, :].astype(F32)
        v[i] = p.v_ref[rows, :]
        rows8[i] = p.gr_ref[p.grp]
        cols8 = jnp.transpose(
            jnp.concatenate([rows8[i], jnp.zeros((GROUP - 8, GROUP), F32)], axis=0))
        gcol[i] = cols8[:, _ROW_GC + p.dirn:_ROW_GC + p.dirn + 1]
        bcol[i] = cols8[:, _ROW_BETA + p.dirn:_ROW_BETA + p.dirn + 1]
        ecol[i] = jnp.exp(gcol[i])
        kt[i] = jnp.transpose(kf)
        kb[i] = kf * bcol[i]
        lhs = jnp.concatenate([q[i], kb[i].astype(BF16)], axis=0)
        a[i] = _dot(lhs, kt[i].astype(BF16))
        yield

    qk, mf, t, nb = ([None] * len(probs) for _ in range(4))
    for i, p in enumerate(probs):
        grow = rows8[i][_ROW_GC + p.dirn:_ROW_GC + p.dirn + 1, :]
        decay = jnp.exp((gcol[i] - grow) + mask_ref[p.dirn])
        qk[i] = (a[i][0:GROUP] * decay).astype(BF16)
        mf[i] = a[i][GROUP:2 * GROUP] * decay
        n1 = mf[i] * mask_ref[2]
        nb[i] = n1.astype(BF16)
        t[i] = mask_ref[5] + n1
        yield

    for i in n:
        nb[i] = _dot(nb[i], nb[i]).astype(BF16)
        yield
    for _ in range(2):
        for i in n:
            y = _dot(jnp.concatenate([nb[i], t[i].astype(BF16)], axis=0), nb[i])
            nb[i] = y[0:GROUP].astype(BF16)
            t[i] = t[i] + y[GROUP:2 * GROUP]
            yield
    for i in n:
        t[i] = t[i] + _dot(t[i].astype(BF16), nb[i])
        yield
    tb, xm = [None] * len(probs), [None] * len(probs)
    for lvl in (3, 4):
        for i in n:
            tb[i] = t[i].astype(BF16)
            xm[i] = _dot((mf[i] * mask_ref[lvl]).astype(BF16), tb[i]).astype(BF16)
            yield
        for i in n:
            t[i] = t[i] - _dot(tb[i], xm[i])
            yield

    wu = [None] * len(probs)
    for i in n:
        rhs = jnp.concatenate(
            [(kb[i] * ecol[i]).astype(BF16), (v[i].astype(F32) * bcol[i]).astype(BF16)], axis=1)
        wu[i] = _dot(t[i].astype(BF16), rhs).astype(BF16)
        yield

    for i, p in enumerate(probs):
        gc_row = rows8[i][_ROW_GC + p.dirn:_ROW_GC + p.dirn + 1, :]
        gt_row = rows8[i][_ROW_GTOT + p.dirn:_ROW_GTOT + p.dirn + 1, :]
        kdec_t = kt[i] * jnp.exp(gt_row - gc_row)
        stack = [(kdec_t * chunk_cols[ci]).astype(BF16) for ci in range(CHUNKS_PER_GROUP)]
        if p.need_o:
            stack.append(qk[i])
        y = _dot(jnp.concatenate(stack, axis=0), wu[i])
        for ci in range(CHUNKS_PER_GROUP):
            z = y[ci * HEAD_DIM:(ci + 1) * HEAD_DIM]
            w2q_ref[p.dirn, p.ch0 + ci, 0:HEAD_DIM, :] = z[:, 0:HEAD_DIM].astype(BF16)
            r_ref[p.dirn, p.ch0 + ci] = z[:, HEAD_DIM:2 * HEAD_DIM]
            d = jnp.exp(gt_row[:, ci * CHUNK:ci * CHUNK + 1])
            d_ref[p.dirn, p.ch0 + ci] = jnp.broadcast_to(d, (8, HEAD_DIM))
        if p.need_o:
            x2 = y[CHUNKS_PER_GROUP * HEAD_DIM:]
            qe = (q[i].astype(F32) * ecol[i] - x2[:, 0:HEAD_DIM]).astype(BF16)
            for ci in range(CHUNKS_PER_GROUP):
                rs = slice(ci * CHUNK, (ci + 1) * CHUNK)
                w2q_ref[p.dirn, p.ch0 + ci, HEAD_DIM:HEAD_DIM + CHUNK, :] = qe[rs]
            o0_ref[p.dirn, pl.ds(p.row0, GROUP), :] = x2[:, HEAD_DIM:2 * HEAD_DIM]
        yield


def _interleave(staged, n_slots, steps):
    done = 0
    slot = 0
    for slot, _ in enumerate(staged, start=1):
        want = slot * len(steps) // n_slots
        while done < want:
            steps[done]()
            done += 1
    assert slot == n_slots and done == len(steps)


def _gdn_kernel(qc_ref, kc_ref, vc_ref, grc_ref, qx_ref, kx_ref, vx_ref, grx_ref,
                o_ref, mask_ref, w2q_ref, r_ref, d_ref, o0_ref, *, n_ctx_chunks, n_lat_chunks):
    r = lax.broadcasted_iota(jnp.int32, (GROUP, GROUP), 0)
    c = lax.broadcasted_iota(jnp.int32, (GROUP, GROUP), 1)
    same64 = (r // CHUNK) == (c // CHUNK)
    same32 = (r // 32) == (c // 32)
    same16 = (r // 16) == (c // 16)
    neg_inf = -jnp.inf
    mask_ref[0] = jnp.where(same64 & (r >= c), 0.0, neg_inf)
    mask_ref[1] = jnp.where(same64 & (r <= c), 0.0, neg_inf)
    mask_ref[2] = jnp.where(same16 & (r != c), -1.0, 0.0)
    mask_ref[3] = jnp.where(same32 & jnp.logical_not(same16), 1.0, 0.0)
    mask_ref[4] = jnp.where(jnp.logical_not(same32), 1.0, 0.0)
    mask_ref[5] = jnp.where(r == c, 1.0, 0.0)

    n_lat_groups = n_lat_chunks // CHUNKS_PER_GROUP
    steps_per_iter = CHUNKS_PER_GROUP * PREP_GROUPS
    n_iter = n_lat_chunks // steps_per_iter

    def lat_probs(it):
        probs = []
        for sub in range(PREP_GROUPS):
            fwd_grp = it * PREP_GROUPS + sub
            for dirn, grp in ((0, fwd_grp), (1, n_lat_groups - 1 - fwd_grp)):
                probs.append(_Prob(qx_ref, kx_ref, vx_ref, grx_ref,
                                   _aligned(grp * GROUP, GROUP), grp,
                                   n_ctx_chunks + grp * CHUNKS_PER_GROUP, dirn, True))
        return probs

    def prep(probs, steps=()):
        _interleave(_gdn_prep(probs, mask_ref, w2q_ref, r_ref, d_ref, o0_ref),
                    PREP_STAGES * len(probs), list(steps))

    def scan_step(ch_base, n_chunks, need_o, i, states):
        new_states = []
        for dirn in range(N_DIR):
            s = states[dirn]
            ci = i if dirn == 0 else n_chunks - 1 - i
            ch = ch_base + ci
            sb = s.astype(BF16)
            if need_o:
                y = _dot(w2q_ref[dirn, ch], sb)
                row = _aligned(ci * CHUNK, CHUNK)
                o0_ref[dirn, pl.ds(row, CHUNK), :] += y[HEAD_DIM:HEAD_DIM + CHUNK]
                y = y[0:HEAD_DIM]
            else:
                y = _dot(w2q_ref[dirn, ch, 0:HEAD_DIM, :], sb)
            new_states.append(d_ref[dirn, ch, 0:1, :] * s + r_ref[dirn, ch] - y)
        return tuple(new_states)

    def lat_scan_steps(it, holder):
        def step(j):
            holder[0] = scan_step(n_ctx_chunks, n_lat_chunks, True, it * steps_per_iter + j,
                                  holder[0])
        return [functools.partial(step, j) for j in range(steps_per_iter)]

    ctx_probs = [_Prob(qc_ref, kc_ref, vc_ref, grc_ref, grp * GROUP, grp,
                       grp * CHUNKS_PER_GROUP, dirn, False)
                 for grp in range(n_ctx_chunks // CHUNKS_PER_GROUP) for dirn in range(N_DIR)]
    prep(ctx_probs + lat_probs(0))
    zero = jnp.zeros((HEAD_DIM, HEAD_DIM), F32)
    states = (zero, zero)
    for i in range(n_ctx_chunks):
        states = scan_step(0, n_ctx_chunks, False, i, states)

    def body(it, states):
        holder = [states]
        prep(lat_probs(it), lat_scan_steps(it - 1, holder))
        return holder[0]

    holder = [lax.fori_loop(1, n_iter, body, states)]
    for step in lat_scan_steps(n_iter - 1, holder):
        step()
    o_ref[...] = (o0_ref[0] + o0_ref[1]).astype(o_ref.dtype)


def _gdn(qc, kc, vc, grc, qx, kx, vx, grx):
    b, nh, lc, _ = qc.shape
    lx = qx.shape[2]
    n_ctx_chunks = lc // CHUNK
    n_lat_chunks = lx // CHUNK
    n_chunks = n_ctx_chunks + n_lat_chunks

    def seq_spec(length):
        return pl.BlockSpec((None, None, length, HEAD_DIM), lambda bi, hi: (bi, hi, 0, 0))

    def gr_spec(length):
        return pl.BlockSpec((None, None, length // GROUP, 8, GROUP),
                            lambda bi, hi: (bi, hi, 0, 0, 0))

    return pl.pallas_call(
        functools.partial(_gdn_kernel, n_ctx_chunks=n_ctx_chunks, n_lat_chunks=n_lat_chunks),
        grid=(b, nh),
        in_specs=[seq_spec(lc), seq_spec(lc), seq_spec(lc), gr_spec(lc),
                  seq_spec(lx), seq_spec(lx), seq_spec(lx), gr_spec(lx)],
        out_specs=seq_spec(lx),
        out_shape=jax.ShapeDtypeStruct((b, nh, lx, HEAD_DIM), BF16),
        scratch_shapes=[
            pltpu.VMEM((6, GROUP, GROUP), F32),
            pltpu.VMEM((N_DIR, n_chunks, HEAD_DIM + CHUNK, HEAD_DIM), BF16),
            pltpu.VMEM((N_DIR, n_chunks, HEAD_DIM, HEAD_DIM), F32),
            pltpu.VMEM((N_DIR, n_chunks, 8, HEAD_DIM), F32),
            pltpu.VMEM((N_DIR, lx, HEAD_DIM), F32),
        ],
        compiler_params=pltpu.CompilerParams(
            dimension_semantics=("arbitrary", "arbitrary"),
            vmem_limit_bytes=52 * MIB),
        name="gdn",
    )(qc, kc, vc, grc, qx, kx, vx, grx)


def _merge_kernel(x_ref, mod_ref, o_ref, wza_ref, wxin_ref, wbg_ref, wcg_ref, wzb_ref,
                  wga_ref, wgb_ref, wa_ref, wb_ref, wout_ref, bias_ref, onw_ref,
                  cbw_ref, cbb_ref, lng_ref, lnb_ref, out_ref):
    x = x_ref[...]
    m = mod_ref[...]
    h = _layernorm(x) * (1.0 + m[1:2, :]) + m[0:1, :]
    hb = h.astype(BF16)
    bias = bias_ref[...]

    sz = _silu(_dot(hb, wza_ref[...]) + bias[0:1, :])
    onw = onw_ref[...]
    pieces = []
    for hd in range(N_HEADS):
        o = o_ref[hd].astype(F32)
        o = o * lax.rsqrt(jnp.mean(o * o, axis=-1, keepdims=True) + RMS_EPS) * onw
        pieces.append((o * sz[:, hd * HEAD_DIM:(hd + 1) * HEAD_DIM]).astype(BF16))
    ya = jnp.concatenate(pieces, axis=1)
    merged = jax.nn.sigmoid(_dot(hb, wga_ref[...]) + bias[5:6, :]) * _dot(ya, wa_ref[...])

    acc = jnp.zeros((x.shape[0], D_MODEL), F32)
    for j in range(WIDTH_B // MERGE_COLS):
        cols = slice(j * MERGE_COLS, (j + 1) * MERGE_COLS)
        xin = _dot(hb, wxin_ref[:, cols]) + bias[1:2, cols]
        bg = _dot(hb, wbg_ref[:, cols]) + bias[2:3, cols]
        cg = _dot(hb, wcg_ref[:, cols]) + bias[3:4, cols]
        zb = _dot(hb, wzb_ref[:, cols]) + bias[4:5, cols]
        conv = _row_conv3(cg * xin, cbw_ref[:, cols], GRID_W) + cbb_ref[:, cols]
        yb = bg * conv * _silu(zb)
        acc = acc + _dot(yb.astype(BF16), wb_ref[cols, :])
    merged = merged + jax.nn.sigmoid(_dot(hb, wgb_ref[...]) + bias[6:7, :]) * acc

    y = _dot(merged.astype(BF16), wout_ref[...])
    res = DN_ALPHA * x + m[2:3, :] * y
    out_ref[...] = _layernorm(res) * lng_ref[...] + lnb_ref[...]


def _merge(x, mod3, o, weights, bias7, onw, cbw, cbb, lng, lnb):
    b, length, _ = x.shape
    tile = MERGE_TILE
    const = lambda shape: pl.BlockSpec(shape, lambda bi, i: (0,) * len(shape),
                                       pipeline_mode=pl.Buffered(1))
    sq = const((D_MODEL, D_MODEL))
    return pl.pallas_call(
        _merge_kernel,
        grid=(b, length // tile),
        in_specs=[
            pl.BlockSpec((None, tile, D_MODEL), lambda bi, i: (bi, i, 0)),
            pl.BlockSpec((None, 3, D_MODEL), lambda bi, i: (bi, 0, 0)),
            pl.BlockSpec((None, N_HEADS, tile, HEAD_DIM), lambda bi, i: (bi, 0, i, 0)),
        ] + [sq] * 10 + [
            const((7, D_MODEL)),
            const((1, HEAD_DIM)),
            const((3, WIDTH_B)),
            const((1, WIDTH_B)),
            const((1, D_MODEL)),
            const((1, D_MODEL)),
        ],
        out_specs=pl.BlockSpec((None, tile, D_MODEL), lambda bi, i: (bi, i, 0)),
        out_shape=jax.ShapeDtypeStruct((b, length, D_MODEL), F32),
        compiler_params=pltpu.CompilerParams(
            dimension_semantics=("arbitrary", "arbitrary"),
            vmem_limit_bytes=56 * MIB),
        name="merge",
    )(x, mod3, o, *weights, bias7, onw, cbw, cbb, lng, lnb)


def _decay_rows(gb, length):
    b = gb.shape[0]
    t = gb[:, :, :3 * N_AB].reshape(b, length, 3 * N_DIR, N_HEADS)
    t = jnp.pad(t, ((0, 0), (0, 0), (0, 8 - 3 * N_DIR), (0, 0)))
    t = jnp.transpose(t, (0, 3, 2, 1)).reshape(b, N_HEADS, 8, length // GROUP, GROUP)
    return jnp.transpose(t, (0, 1, 3, 2, 4))


def kernel(x, c, ctx, c_ctx, w_mod, b_mod, w_in, b_in, conv_qkv_w, a_log, dt_bias, o_norm_w,
           conv_b_w, conv_b_b, w_a, w_b, w_out, ln_g, ln_b):
    assert w_mod.shape[0] == DEPTH, "single-layer block only"
    batch, seq, _ = x.shape
    ctx_len = ctx.shape[1]
    assert seq % PROJ_TILE == 0 and seq % MERGE_TILE == 0 and ctx_len % GROUP == 0

    c16 = jnp.zeros((16, D_MODEL), F32).at[:batch].set(c).at[batch].set(c_ctx)
    mod = _modulation(c16, w_mod[0], b_mod[0][None, :])
    mod3 = mod.reshape(16, 3, D_MODEL)

    w = w_in[0]
    bi = b_in[0]
    o_qkv, o_za, o_ab = 0, 3 * WIDTH_A, 4 * WIDTH_A
    o_rest = o_ab + 2 * N_AB
    seg = lambda arr, k: arr[..., o_rest + k * 1024:o_rest + (k + 1) * 1024]
    wqkv = w[:, o_qkv:o_za].astype(BF16)
    bqkv = bi[None, o_qkv:o_za]
    wab = jnp.zeros((D_MODEL, LANES), F32).at[:, :2 * N_AB].set(w[:, o_ab:o_rest]).astype(BF16)
    bab = jnp.zeros((1, LANES), F32).at[0, :2 * N_AB].set(bi[o_ab:o_rest])
    par = jnp.zeros((2, LANES), F32)
    par = par.at[0, :N_AB].set(a_log[0].reshape(-1)).at[1, :N_AB].set(dt_bias[0].reshape(-1))
    cw = conv_qkv_w[0]

    qx, kx, vx, gbx = _projection(x, mod3, None, wqkv, bqkv, wab, bab, cw, par,
                                  tile=PROJ_TILE, row_len=GRID_W)
    qc, kc, vc, gbc = _projection(ctx, mod3, batch, wqkv, bqkv, wab, bab, cw, par,
                                  tile=ctx_len, row_len=ctx_len)
    o = _gdn(qc, kc, vc, _decay_rows(gbc, ctx_len), qx, kx, vx, _decay_rows(gbx, seq))

    w_za = w[:, o_za:o_ab].astype(BF16)
    rest_w = [seg(w, k).astype(BF16) for k in range(6)]
    weights = [w_za, rest_w[0], rest_w[1], rest_w[2], rest_w[3], rest_w[4], rest_w[5],
               w_a[0].astype(BF16), w_b[0].astype(BF16), w_out[0].astype(BF16)]
    bias7 = jnp.stack([bi[o_za:o_ab]] + [seg(bi, k) for k in range(6)], axis=0)
    return _merge(x, mod3, o, weights, bias7, o_norm_w[0][None, :], conv_b_w[0],
                  conv_b_b[0][None, :], ln_g[0][None, :], ln_b[0][None, :])
```

```python
import functools
from typing import NamedTuple

import jax
import jax.numpy as jnp
from jax import lax
from jax.experimental import pallas as pl
from jax.experimental.pallas import tpu as pltpu

D_MODEL = 1024
N_HEADS = 8
HEAD_DIM = 128
WIDTH_A = N_HEADS * HEAD_DIM
WIDTH_B = 1024
N_DIR = 2
CHUNK = 64
GRID_W = 64
DEPTH = 1
DN_ALPHA = (2.0 * DEPTH) ** 0.25
LN_EPS = 1e-5
RMS_EPS = 1e-6
L2_EPS = 1e-6

GROUP = 128
CHUNKS_PER_GROUP = GROUP // CHUNK
LANES = 128
N_AB = N_DIR * N_HEADS

PROJ_TILE = 512
PROJ_COLS = 512
MERGE_TILE = 512
MERGE_COLS = 256
MOD_COLS = 512
PREP_GROUPS = 8
MIB = 1024 * 1024

F32 = jnp.float32
BF16 = jnp.bfloat16


def _dot(a, b):
    return jnp.dot(a, b, preferred_element_type=F32)


def _aligned(v, m):
    return v if isinstance(v, int) else pl.multiple_of(v, m)


def _split3(a):
    hi = a.astype(BF16)
    r = a - hi.astype(F32)
    mid = r.astype(BF16)
    lo = (r - mid.astype(F32)).astype(BF16)
    return hi, mid, lo


def _silu(z):
    return z * jax.nn.sigmoid(z)


def _layernorm(x):
    mu = jnp.mean(x, axis=-1, keepdims=True)
    xc = x - mu
    var = jnp.mean(xc * xc, axis=-1, keepdims=True)
    return xc * lax.rsqrt(var + LN_EPS)


def _row_conv3(u, w, row_len):
    n = u.shape[0]
    pos = lax.broadcasted_iota(jnp.int32, (n, 1), 0) & (row_len - 1)
    has_prev = (pos != 0).astype(F32)
    has_next = (pos != row_len - 1).astype(F32)
    prev = pltpu.roll(u, 1, 0) * has_prev
    nxt = pltpu.roll(u, n - 1, 0) * has_next
    return prev * w[0:1, :] + u * w[1:2, :] + nxt * w[2:3, :]


def _mod_kernel(c_ref, w_ref, b_ref, o_ref):
    s = _silu(c_ref[...])
    s_hi, s_mid, _ = _split3(s)
    w = w_ref[...]
    w_hi, w_mid, _ = _split3(w)
    o_ref[...] = _dot(s_hi, w_hi) + _dot(s_hi, w_mid) + _dot(s_mid, w_hi) + b_ref[...]


def _modulation(c16, w_mod, b_mod):
    n_out = w_mod.shape[1]
    return pl.pallas_call(
        _mod_kernel,
        grid=(n_out // MOD_COLS,),
        in_specs=[
            pl.BlockSpec((16, D_MODEL), lambda j: (0, 0)),
            pl.BlockSpec((D_MODEL, MOD_COLS), lambda j: (0, j)),
            pl.BlockSpec((1, MOD_COLS), lambda j: (0, j)),
        ],
        out_specs=pl.BlockSpec((16, MOD_COLS), lambda j: (0, j)),
        out_shape=jax.ShapeDtypeStruct((16, n_out), F32),
        name="mod",
    )(c16, w_mod, b_mod)


def _proj_kernel(x_ref, mod_ref, wqkv_ref, bqkv_ref, wab_ref, bab_ref, cw_ref, par_ref,
                 q_ref, k_ref, v_ref, gb_ref, *, tile, row_len):
    m = mod_ref[...]
    h = _layernorm(x_ref[...]) * (1.0 + m[1:2, :]) + m[0:1, :]
    hb = h.astype(BF16)

    sr = lax.broadcasted_iota(jnp.int32, (2 * row_len, row_len), 0)
    sc = lax.broadcasted_iota(jnp.int32, (2 * row_len, row_len), 1)
    src_row = jnp.where(sr < row_len, sr - 1, sr - row_len + 1)
    shift2 = jnp.where(sc == src_row, 1.0, 0.0).astype(BF16)

    heads_per_pass = PROJ_COLS // HEAD_DIM
    outs = (q_ref, k_ref, v_ref)
    for j in range(3 * WIDTH_A // PROJ_COLS):
        cols = slice(j * PROJ_COLS, (j + 1) * PROJ_COLS)
        p = _dot(hb, wqkv_ref[:, cols]) + bqkv_ref[:, cols]
        pb = p.astype(BF16)
        taps = [_dot(shift2, pb[r0:r0 + row_len]) for r0 in range(0, tile, row_len)]
        prev = jnp.concatenate([t[0:row_len] for t in taps], axis=0)
        nxt = jnp.concatenate([t[row_len:2 * row_len] for t in taps], axis=0)
        cwj = cw_ref[:, cols]
        s = _silu(prev * cwj[0:1, :] + p * cwj[1:2, :] + nxt * cwj[2:3, :])
        which = (j * PROJ_COLS) // WIDTH_A
        head0 = ((j * PROJ_COLS) % WIDTH_A) // HEAD_DIM
        for hh in range(heads_per_pass):
            sl = s[:, hh * HEAD_DIM:(hh + 1) * HEAD_DIM]
            if which < 2:
                sl = sl * lax.rsqrt(jnp.sum(sl * sl, axis=-1, keepdims=True) + L2_EPS)
            if which == 0:
                sl = sl * (HEAD_DIM ** -0.5)
            outs[which][head0 + hh] = sl.astype(BF16)

    pab = _dot(hb, wab_ref[...]) + bab_ref[...]
    par = par_ref[...]
    z = pab + par[1:2, :]
    softplus = jnp.maximum(z, 0.0) + jnp.log1p(jnp.exp(-jnp.abs(z)))
    g = -jnp.exp(par[0:1, :]) * softplus
    beta = jax.nn.sigmoid(pab)

    g3 = jnp.concatenate(_split3(g), axis=1)
    r = lax.broadcasted_iota(jnp.int32, (GROUP, GROUP), 0)
    c = lax.broadcasted_iota(jnp.int32, (GROUP, GROUP), 1)
    same = (r // CHUNK) == (c // CHUNK)
    tri_f = (same & (r >= c)).astype(BF16)
    tri_b = (same & (r <= c)).astype(BF16)
    lane = lax.broadcasted_iota(jnp.int32, (GROUP, LANES), 1)
    for blk in range(tile // GROUP):
        rows = slice(blk * GROUP, (blk + 1) * GROUP)
        pf = _dot(tri_f, g3[rows])
        pb = _dot(tri_b, g3[rows])
        pf = pf[:, 0:LANES] + pf[:, LANES:2 * LANES] + pf[:, 2 * LANES:3 * LANES]
        pb = pb[:, 0:LANES] + pb[:, LANES:2 * LANES] + pb[:, 2 * LANES:3 * LANES]
        gtot = pf + pb - g[rows]
        gc = jnp.where(lane < N_HEADS, pf, pb)
        gb_ref[rows, :] = jnp.where(
            lane < N_AB, gc,
            jnp.where(lane < 2 * N_AB, beta[rows],
                      jnp.where(lane < 3 * N_AB, pltpu.roll(gtot, 2 * N_AB, 1), 0.0)))


def _projection(x, mod3, mod_row, wqkv, bqkv, wab, bab, cw, par, *, tile, row_len):
    b, length, _ = x.shape
    const = lambda shape: pl.BlockSpec(shape, lambda bi, i: (0,) * len(shape),
                                       pipeline_mode=pl.Buffered(1))
    if mod_row is None:
        mod_spec = pl.BlockSpec((None, 3, D_MODEL), lambda bi, i: (bi, 0, 0))
    else:
        mod_spec = pl.BlockSpec((None, 3, D_MODEL), lambda bi, i: (mod_row, 0, 0))
    head_spec = pl.BlockSpec((None, N_HEADS, tile, HEAD_DIM), lambda bi, i: (bi, 0, i, 0))
    head_shape = jax.ShapeDtypeStruct((b, N_HEADS, length, HEAD_DIM), BF16)
    return pl.pallas_call(
        functools.partial(_proj_kernel, tile=tile, row_len=row_len),
        grid=(b, length // tile),
        in_specs=[
            pl.BlockSpec((None, tile, D_MODEL), lambda bi, i: (bi, i, 0)),
            mod_spec,
            const((D_MODEL, 3 * WIDTH_A)),
            const((1, 3 * WIDTH_A)),
            const((D_MODEL, LANES)),
            const((1, LANES)),
            const((3, 3 * WIDTH_A)),
            const((2, LANES)),
        ],
        out_specs=[head_spec, head_spec, head_spec,
                   pl.BlockSpec((None, tile, LANES), lambda bi, i: (bi, i, 0))],
        out_shape=[head_shape, head_shape, head_shape,
                   jax.ShapeDtypeStruct((b, length, LANES), F32)],
        compiler_params=pltpu.CompilerParams(
            dimension_semantics=("arbitrary", "arbitrary"),
            vmem_limit_bytes=48 * MIB),
        name="proj",
    )(x, mod3, wqkv, bqkv, wab, bab, cw, par)


class _Prob(NamedTuple):
    q_ref: object
    k_ref: object
    v_ref: object
    gr_ref: object
    row0: object
    grp: object
    ch0: object
    dirn: int
    need_o: bool


_ROW_GC, _ROW_BETA, _ROW_GTOT = 0, 2, 4


PREP_STAGES = 12


def _gdn_prep(probs, mask_ref, w2q_ref, r_ref, d_ref, o0_ref):
    n = range(len(probs))
    lane = lax.broadcasted_iota(jnp.int32, (1, GROUP), 1)
    chunk_cols = [((lane // CHUNK) == ci).astype(F32) for ci in range(CHUNKS_PER_GROUP)]
    q, kt, kb, v, rows8, gcol, bcol, ecol, a = ([None] * len(probs) for _ in range(9))
    for i, p in enumerate(probs):
        rows = pl.ds(p.row0, GROUP)
        q[i] = p.q_ref[rows, :]
        kf = p.k_ref[rows, :].astype(F32)
        v[i] = p.v_ref[rows, :]
        rows8[i] = p.gr_ref[p.grp]
        cols8 = jnp.transpose(
            jnp.concatenate([rows8[i], jnp.zeros((GROUP - 8, GROUP), F32)], axis=0))
        gcol[i] = cols8[:, _ROW_GC + p.dirn:_ROW_GC + p.dirn + 1]
        bcol[i] = cols8[:, _ROW_BETA + p.dirn:_ROW_BETA + p.dirn + 1]
        ecol[i] = jnp.exp(gcol[i])
        kt[i] = jnp.transpose(kf)
        kb[i] = kf * bcol[i]
        lhs = jnp.concatenate([q[i], kb[i].astype(BF16)], axis=0)
        a[i] = _dot(lhs, kt[i].astype(BF16))
        yield

    qk, mf, t, nb = ([None] * len(probs) for _ in range(4))
    for i, p in enumerate(probs):
        grow = rows8[i][_ROW_GC + p.dirn:_ROW_GC + p.dirn + 1, :]
        decay = jnp.exp((gcol[i] - grow) + mask_ref[p.dirn])
        qk[i] = (a[i][0:GROUP] * decay).astype(BF16)
        mf[i] = a[i][GROUP:2 * GROUP] * decay
        n1 = mf[i] * mask_ref[2]
        nb[i] = n1.astype(BF16)
        t[i] = mask_ref[5] + n1
        yield

    for i in n:
        nb[i] = _dot(nb[i], nb[i]).astype(BF16)
        yield
    for _ in range(2):
        for i in n:
            y = _dot(jnp.concatenate([nb[i], t[i].astype(BF16)], axis=0), nb[i])
            nb[i] = y[0:GROUP].astype(BF16)
            t[i] = t[i] + y[GROUP:2 * GROUP]
            yield
    for i in n:
        t[i] = t[i] + _dot(t[i].astype(BF16), nb[i])
        yield
    tb, xm = [None] * len(probs), [None] * len(probs)
    for lvl in (3, 4):
        for i in n:
            tb[i] = t[i].astype(BF16)
            xm[i] = _dot((mf[i] * mask_ref[lvl]).astype(BF16), tb[i]).astype(BF16)
            yield
        for i in n:
            t[i] = t[i] - _dot(tb[i], xm[i])
            yield

    wu = [None] * len(probs)
    for i in n:
        rhs = jnp.concatenate(
            [(kb[i] * ecol[i]).astype(BF16), (v[i].astype(F32) * bcol[i]).astype(BF16)], axis=1)
        wu[i] = _dot(t[i].astype(BF16), rhs).astype(BF16)
        yield

    for i, p in enumerate(probs):
        gc_row = rows8[i][_ROW_GC + p.dirn:_ROW_GC + p.dirn + 1, :]
        gt_row = rows8[i][_ROW_GTOT + p.dirn:_ROW_GTOT + p.dirn + 1, :]
        kdec_t = kt[i] * jnp.exp(gt_row - gc_row)
        stack = [(kdec_t * chunk_cols[ci]).astype(BF16) for ci in range(CHUNKS_PER_GROUP)]
        if p.need_o:
            stack.append(qk[i])
        y = _dot(jnp.concatenate(stack, axis=0), wu[i])
        for ci in range(CHUNKS_PER_GROUP):
            z = y[ci * HEAD_DIM:(ci + 1) * HEAD_DIM]
            w2q_ref[p.dirn, p.ch0 + ci, 0:HEAD_DIM, :] = z[:, 0:HEAD_DIM].astype(BF16)
            r_ref[p.dirn, p.ch0 + ci] = z[:, HEAD_DIM:2 * HEAD_DIM].astype(BF16)
            d = jnp.exp(gt_row[:, ci * CHUNK:ci * CHUNK + 1])
            d_ref[p.dirn, p.ch0 + ci] = jnp.broadcast_to(d, (8, HEAD_DIM))
        if p.need_o:
            x2 = y[CHUNKS_PER_GROUP * HEAD_DIM:]
            qe = (q[i].astype(F32) * ecol[i] - x2[:, 0:HEAD_DIM]).astype(BF16)
            for ci in range(CHUNKS_PER_GROUP):
                rs = slice(ci * CHUNK, (ci + 1) * CHUNK)
                w2q_ref[p.dirn, p.ch0 + ci, HEAD_DIM:HEAD_DIM + CHUNK, :] = qe[rs]
            o0_ref[p.dirn, pl.ds(p.row0, GROUP), :] = x2[:, HEAD_DIM:2 * HEAD_DIM]
        yield


def _interleave(staged, n_slots, steps, late_steps=(), late_from=0):
    done = 0
    late_done = 0
    slot = 0
    for slot, _ in enumerate(staged, start=1):
        want = slot * len(steps) // n_slots
        while done < want:
            steps[done]()
            done += 1
        if slot > late_from and late_done < len(late_steps):
            late_steps[late_done]()
            late_done += 1
    assert slot == n_slots and done == len(steps) and late_done == len(late_steps)


def _gdn_kernel(qc_ref, kc_ref, vc_ref, grc_ref, qx_ref, kx_ref, vx_ref, grx_ref,
                o_ref, mask_ref, w2q_ref, r_ref, d_ref, o0_ref, state_ref,
                *, n_ctx_chunks, n_lat_chunks, n_seqs):
    step_id = pl.program_id(0)

    @pl.when(step_id == 0)
    def _():
        r = lax.broadcasted_iota(jnp.int32, (GROUP, GROUP), 0)
        c = lax.broadcasted_iota(jnp.int32, (GROUP, GROUP), 1)
        same64 = (r // CHUNK) == (c // CHUNK)
        same32 = (r // 32) == (c // 32)
        same16 = (r // 16) == (c // 16)
        neg_inf = -jnp.inf
        mask_ref[0] = jnp.where(same64 & (r >= c), 0.0, neg_inf)
        mask_ref[1] = jnp.where(same64 & (r <= c), 0.0, neg_inf)
        mask_ref[2] = jnp.where(same16 & (r != c), -1.0, 0.0)
        mask_ref[3] = jnp.where(same32 & jnp.logical_not(same16), 1.0, 0.0)
        mask_ref[4] = jnp.where(jnp.logical_not(same32), 1.0, 0.0)
        mask_ref[5] = jnp.where(r == c, 1.0, 0.0)
        w2q_ref[...] = jnp.zeros(w2q_ref.shape, w2q_ref.dtype)
        r_ref[...] = jnp.zeros(r_ref.shape, r_ref.dtype)
        d_ref[...] = jnp.zeros(d_ref.shape, d_ref.dtype)
        o0_ref[...] = jnp.zeros(o0_ref.shape, o0_ref.dtype)
        state_ref[...] = jnp.zeros(state_ref.shape, state_ref.dtype)

    o0_cur = o0_ref.at[step_id % 2]
    o0_prev = o0_ref.at[(step_id + 1) % 2]
    n_lat_groups = n_lat_chunks // CHUNKS_PER_GROUP
    steps_per_iter = CHUNKS_PER_GROUP * PREP_GROUPS
    n_iter = n_lat_chunks // steps_per_iter

    def lat_probs(it):
        probs = []
        for sub in range(PREP_GROUPS):
            fwd_grp = it * PREP_GROUPS + sub
            for dirn, grp in ((0, fwd_grp), (1, n_lat_groups - 1 - fwd_grp)):
                probs.append(_Prob(qx_ref, kx_ref, vx_ref, grx_ref,
                                   _aligned(grp * GROUP, GROUP), grp,
                                   n_ctx_chunks + grp * CHUNKS_PER_GROUP, dirn, True))
        return probs

    def prep(probs, steps=(), late_steps=(), late_from=0):
        _interleave(_gdn_prep(probs, mask_ref, w2q_ref, r_ref, d_ref, o0_cur),
                    PREP_STAGES * len(probs), list(steps), list(late_steps), late_from)

    def scan_step(o0_view, ch_base, n_chunks, i, states):
        new_states = []
        for dirn in range(N_DIR):
            s = states[dirn]
            ci = i if dirn == 0 else n_chunks - 1 - i
            ch = ch_base + ci
            sb = s.astype(BF16)
            if o0_view is not None:
                y = _dot(w2q_ref[dirn, ch], sb)
                row = _aligned(ci * CHUNK, CHUNK)
                o0_view[dirn, pl.ds(row, CHUNK), :] += y[HEAD_DIM:HEAD_DIM + CHUNK]
                y = y[0:HEAD_DIM]
            else:
                y = _dot(w2q_ref[dirn, ch, 0:HEAD_DIM, :], sb)
            new_states.append(d_ref[dirn, ch, 0:1, :] * s + r_ref[dirn, ch].astype(F32) - y)
        return tuple(new_states)

    def lat_scan_steps(o0_view, it, holder):
        def step(j):
            holder[0] = scan_step(o0_view, n_ctx_chunks, n_lat_chunks, it * steps_per_iter + j,
                                  holder[0])
        return [functools.partial(step, j) for j in range(steps_per_iter)]

    ctx_probs = [_Prob(qc_ref, kc_ref, vc_ref, grc_ref, grp * GROUP, grp,
                       grp * CHUNKS_PER_GROUP, dirn, False)
                 for grp in range(n_ctx_chunks // CHUNKS_PER_GROUP) for dirn in range(N_DIR)]
    prev_holder = [(state_ref[0], state_ref[1])]
    zero = jnp.zeros((HEAD_DIM, HEAD_DIM), F32)
    ctx_holder = [(zero, zero)]

    def ctx_step(i):
        ctx_holder[0] = scan_step(None, 0, n_ctx_chunks, i, ctx_holder[0])

    all_probs = ctx_probs + lat_probs(0)
    prep(all_probs, lat_scan_steps(o0_prev, n_iter - 1, prev_holder),
         [functools.partial(ctx_step, i) for i in range(n_ctx_chunks)],
         (PREP_STAGES - 1) * len(all_probs) + len(ctx_probs))
    o_ref[...] = (o0_prev[0] + o0_prev[1]).astype(o_ref.dtype)
    ctx_states = ctx_holder[0]

    @pl.when(step_id < n_seqs)
    def _():
        states = ctx_states

        def body(it, states):
            holder = [states]
            prep(lat_probs(it), lat_scan_steps(o0_cur, it - 1, holder))
            return holder[0]

        states = lax.fori_loop(1, n_iter, body, states)
        state_ref[0] = states[0]
        state_ref[1] = states[1]


def _gdn(qc, kc, vc, grc, qx, kx, vx, grx):
    b, nh, lc, _ = qc.shape
    lx = qx.shape[2]
    n_ctx_chunks = lc // CHUNK
    n_lat_chunks = lx // CHUNK
    n_chunks = n_ctx_chunks + n_lat_chunks

    n_seqs = b * nh

    def cur(i):
        s = jnp.minimum(i, n_seqs - 1)
        return s // nh, s % nh

    def prev(i):
        s = jnp.maximum(i - 1, 0)
        return s // nh, s % nh

    def seq_spec(length, which=cur):
        return pl.BlockSpec((None, None, length, HEAD_DIM), lambda i: (*which(i), 0, 0))

    def gr_spec(length):
        return pl.BlockSpec((None, None, length // GROUP, 8, GROUP),
                            lambda i: (*cur(i), 0, 0, 0))

    return pl.pallas_call(
        functools.partial(_gdn_kernel, n_ctx_chunks=n_ctx_chunks, n_lat_chunks=n_lat_chunks,
                          n_seqs=n_seqs),
        grid=(n_seqs + 1,),
        in_specs=[seq_spec(lc), seq_spec(lc), seq_spec(lc), gr_spec(lc),
                  seq_spec(lx), seq_spec(lx), seq_spec(lx), gr_spec(lx)],
        out_specs=seq_spec(lx, prev),
        out_shape=jax.ShapeDtypeStruct((b, nh, lx, HEAD_DIM), BF16),
        scratch_shapes=[
            pltpu.VMEM((6, GROUP, GROUP), F32),
            pltpu.VMEM((N_DIR, n_chunks, HEAD_DIM + CHUNK, HEAD_DIM), BF16),
            pltpu.VMEM((N_DIR, n_chunks, HEAD_DIM, HEAD_DIM), BF16),
            pltpu.VMEM((N_DIR, n_chunks, 8, HEAD_DIM), F32),
            pltpu.VMEM((2, N_DIR, lx, HEAD_DIM), F32),
            pltpu.VMEM((N_DIR, HEAD_DIM, HEAD_DIM), F32),
        ],
        compiler_params=pltpu.CompilerParams(
            dimension_semantics=("arbitrary",),
            vmem_limit_bytes=52 * MIB),
        name="gdn",
    )(qc, kc, vc, grc, qx, kx, vx, grx)


def _merge_kernel(x_ref, mod_ref, o_ref, wza_ref, wxin_ref, wbg_ref, wcg_ref, wzb_ref,
                  wga_ref, wgb_ref, wa_ref, wb_ref, wout_ref, bias_ref, onw_ref,
                  cbw_ref, cbb_ref, lng_ref, lnb_ref, out_ref):
    x = x_ref[...]
    m = mod_ref[...]
    h = _layernorm(x) * (1.0 + m[1:2, :]) + m[0:1, :]
    hb = h.astype(BF16)
    bias = bias_ref[...]

    sz = _silu(_dot(hb, wza_ref[...]) + bias[0:1, :])
    onw = onw_ref[...]
    pieces = []
    for hd in range(N_HEADS):
        o = o_ref[hd].astype(F32)
        o = o * lax.rsqrt(jnp.mean(o * o, axis=-1, keepdims=True) + RMS_EPS) * onw
        pieces.append((o * sz[:, hd * HEAD_DIM:(hd + 1) * HEAD_DIM]).astype(BF16))
    ya = jnp.concatenate(pieces, axis=1)
    merged = jax.nn.sigmoid(_dot(hb, wga_ref[...]) + bias[5:6, :]) * _dot(ya, wa_ref[...])

    acc = jnp.zeros((x.shape[0], D_MODEL), F32)
    for j in range(WIDTH_B // MERGE_COLS):
        cols = slice(j * MERGE_COLS, (j + 1) * MERGE_COLS)
        xin = _dot(hb, wxin_ref[:, cols]) + bias[1:2, cols]
        bg = _dot(hb, wbg_ref[:, cols]) + bias[2:3, cols]
        cg = _dot(hb, wcg_ref[:, cols]) + bias[3:4, cols]
        zb = _dot(hb, wzb_ref[:, cols]) + bias[4:5, cols]
        conv = _row_conv3(cg * xin, cbw_ref[:, cols], GRID_W) + cbb_ref[:, cols]
        yb = bg * conv * _silu(zb)
        acc = acc + _dot(yb.astype(BF16), wb_ref[cols, :])
    merged = merged + jax.nn.sigmoid(_dot(hb, wgb_ref[...]) + bias[6:7, :]) * acc

    y = _dot(merged.astype(BF16), wout_ref[...])
    res = DN_ALPHA * x + m[2:3, :] * y
    out_ref[...] = _layernorm(res) * lng_ref[...] + lnb_ref[...]


def _merge(x, mod3, o, weights, bias7, onw, cbw, cbb, lng, lnb):
    b, length, _ = x.shape
    tile = MERGE_TILE
    const = lambda shape: pl.BlockSpec(shape, lambda bi, i: (0,) * len(shape),
                                       pipeline_mode=pl.Buffered(1))
    sq = const((D_MODEL, D_MODEL))
    return pl.pallas_call(
        _merge_kernel,
        grid=(b, length // tile),
        in_specs=[
            pl.BlockSpec((None, tile, D_MODEL), lambda bi, i: (bi, i, 0)),
            pl.BlockSpec((None, 3, D_MODEL), lambda bi, i: (bi, 0, 0)),
            pl.BlockSpec((None, N_HEADS, tile, HEAD_DIM), lambda bi, i: (bi, 0, i, 0)),
        ] + [sq] * 10 + [
            const((7, D_MODEL)),
            const((1, HEAD_DIM)),
            const((3, WIDTH_B)),
            const((1, WIDTH_B)),
            const((1, D_MODEL)),
            const((1, D_MODEL)),
        ],
        out_specs=pl.BlockSpec((None, tile, D_MODEL), lambda bi, i: (bi, i, 0)),
        out_shape=jax.ShapeDtypeStruct((b, length, D_MODEL), F32),
        compiler_params=pltpu.CompilerParams(
            dimension_semantics=("arbitrary", "arbitrary"),
            vmem_limit_bytes=56 * MIB),
        name="merge",
    )(x, mod3, o, *weights, bias7, onw, cbw, cbb, lng, lnb)


def _decay_rows(gb, length):
    b = gb.shape[0]
    t = gb[:, :, :3 * N_AB].reshape(b, length, 3 * N_DIR, N_HEADS)
    t = jnp.pad(t, ((0, 0), (0, 0), (0, 8 - 3 * N_DIR), (0, 0)))
    t = jnp.transpose(t, (0, 3, 2, 1)).reshape(b, N_HEADS, 8, length // GROUP, GROUP)
    return jnp.transpose(t, (0, 1, 3, 2, 4))


def kernel(x, c, ctx, c_ctx, w_mod, b_mod, w_in, b_in, conv_qkv_w, a_log, dt_bias, o_norm_w,
           conv_b_w, conv_b_b, w_a, w_b, w_out, ln_g, ln_b):
    assert w_mod.shape[0] == DEPTH, "single-layer block only"
    batch, seq, _ = x.shape
    ctx_len = ctx.shape[1]
    assert seq % PROJ_TILE == 0 and seq % MERGE_TILE == 0 and ctx_len % GROUP == 0

    c16 = jnp.zeros((16, D_MODEL), F32).at[:batch].set(c).at[batch].set(c_ctx)
    mod = _modulation(c16, w_mod[0], b_mod[0][None, :])
    mod3 = mod.reshape(16, 3, D_MODEL)

    w = w_in[0]
    bi = b_in[0]
    o_qkv, o_za, o_ab = 0, 3 * WIDTH_A, 4 * WIDTH_A
    o_rest = o_ab + 2 * N_AB
    seg = lambda arr, k: arr[..., o_rest + k * 1024:o_rest + (k + 1) * 1024]
    wqkv = w[:, o_qkv:o_za].astype(BF16)
    bqkv = bi[None, o_qkv:o_za]
    wab = jnp.zeros((D_MODEL, LANES), F32).at[:, :2 * N_AB].set(w[:, o_ab:o_rest]).astype(BF16)
    bab = jnp.zeros((1, LANES), F32).at[0, :2 * N_AB].set(bi[o_ab:o_rest])
    par = jnp.zeros((2, LANES), F32)
    par = par.at[0, :N_AB].set(a_log[0].reshape(-1)).at[1, :N_AB].set(dt_bias[0].reshape(-1))
    cw = conv_qkv_w[0]

    qx, kx, vx, gbx = _projection(x, mod3, None, wqkv, bqkv, wab, bab, cw, par,
                                  tile=PROJ_TILE, row_len=GRID_W)
    qc, kc, vc, gbc = _projection(ctx, mod3, batch, wqkv, bqkv, wab, bab, cw, par,
                                  tile=ctx_len, row_len=ctx_len)
    o = _gdn(qc, kc, vc, _decay_rows(gbc, ctx_len), qx, kx, vx, _decay_rows(gbx, seq))

    w_za = w[:, o_za:o_ab].astype(BF16)
    rest_w = [seg(w, k).astype(BF16) for k in range(6)]
    weights = [w_za, rest_w[0], rest_w[1], rest_w[2], rest_w[3], rest_w[4], rest_w[5],
               w_a[0].astype(BF16), w_b[0].astype(BF16), w_out[0].astype(BF16)]
    bias7 = jnp.stack([bi[o_za:o_ab]] + [seg(bi, k) for k in range(6)], axis=0)
    return _merge(x, mod3, o, weights, bias7, o_norm_w[0][None, :], conv_b_w[0],
                  conv_b_b[0][None, :], ln_g[0][None, :], ln_b[0][None, :])
```

```python
import functools
from typing import NamedTuple

import jax
import jax.numpy as jnp
from jax import lax
from jax.experimental import pallas as pl
from jax.experimental.pallas import tpu as pltpu

D_MODEL = 1024
N_HEADS = 8
HEAD_DIM = 128
WIDTH_A = N_HEADS * HEAD_DIM
WIDTH_B = 1024
N_DIR = 2
CHUNK = 64
GRID_W = 64
DEPTH = 1
DN_ALPHA = (2.0 * DEPTH) ** 0.25
LN_EPS = 1e-5
RMS_EPS = 1e-6
L2_EPS = 1e-6

GROUP = 128
CHUNKS_PER_GROUP = GROUP // CHUNK
LANES = 128
N_AB = N_DIR * N_HEADS

PROJ_TILE = 512
PROJ_COLS = 512
MERGE_TILE = 512
MERGE_COLS = 256
MOD_COLS = 512
PREP_GROUPS = 8
PREP_WAVE = 4
PREP_LAG = 1
MIB = 1024 * 1024

F32 = jnp.float32
BF16 = jnp.bfloat16


def _dot(a, b):
    return jnp.dot(a, b, preferred_element_type=F32)


def _aligned(v, m):
    return v if isinstance(v, int) else pl.multiple_of(v, m)


def _split3(a):
    hi = a.astype(BF16)
    r = a - hi.astype(F32)
    mid = r.astype(BF16)
    lo = (r - mid.astype(F32)).astype(BF16)
    return hi, mid, lo


def _silu(z):
    half = 0.5 * z
    return half + half * jnp.tanh(half)


def _layernorm(x):
    mu = jnp.mean(x, axis=-1, keepdims=True)
    xc = x - mu
    var = jnp.mean(xc * xc, axis=-1, keepdims=True)
    return xc * lax.rsqrt(var + LN_EPS)


def _row_conv3(u, w, row_len):
    n = u.shape[0]
    pos = lax.broadcasted_iota(jnp.int32, (n, 1), 0) & (row_len - 1)
    has_prev = (pos != 0).astype(F32)
    has_next = (pos != row_len - 1).astype(F32)
    prev = pltpu.roll(u, 1, 0) * has_prev
    nxt = pltpu.roll(u, n - 1, 0) * has_next
    return prev * w[0:1, :] + u * w[1:2, :] + nxt * w[2:3, :]


def _mod_kernel(c_ref, w_ref, b_ref, o_ref):
    s = _silu(c_ref[...])
    s_hi, s_mid, _ = _split3(s)
    w = w_ref[...]
    w_hi, w_mid, _ = _split3(w)
    o_ref[...] = _dot(s_hi, w_hi) + _dot(s_hi, w_mid) + _dot(s_mid, w_hi) + b_ref[...]


def _modulation(c16, w_mod, b_mod):
    n_out = w_mod.shape[1]
    return pl.pallas_call(
        _mod_kernel,
        grid=(n_out // MOD_COLS,),
        in_specs=[
            pl.BlockSpec((16, D_MODEL), lambda j: (0, 0)),
            pl.BlockSpec((D_MODEL, MOD_COLS), lambda j: (0, j)),
            pl.BlockSpec((1, MOD_COLS), lambda j: (0, j)),
        ],
        out_specs=pl.BlockSpec((16, MOD_COLS), lambda j: (0, j)),
        out_shape=jax.ShapeDtypeStruct((16, n_out), F32),
        name="mod",
    )(c16, w_mod, b_mod)


def _proj_kernel(x_ref, mod_ref, wqkv_ref, bqkv_ref, wab_ref, bab_ref, cw_ref, par_ref,
                 q_ref, k_ref, v_ref, gr_ref, tr_ref, *, tile, row_len):
    m = mod_ref[...]
    h = _layernorm(x_ref[...]) * (1.0 + m[1:2, :]) + m[0:1, :]
    hb = h.astype(BF16)

    sr = lax.broadcasted_iota(jnp.int32, (2 * row_len, row_len), 0)
    sc = lax.broadcasted_iota(jnp.int32, (2 * row_len, row_len), 1)
    src_row = jnp.where(sr < row_len, sr - 1, sr - row_len + 1)
    shift2 = jnp.where(sc == src_row, 1.0, 0.0).astype(BF16)

    heads_per_pass = PROJ_COLS // HEAD_DIM
    outs = (q_ref, k_ref, v_ref)
    for j in range(3 * WIDTH_A // PROJ_COLS):
        cols = slice(j * PROJ_COLS, (j + 1) * PROJ_COLS)
        p = _dot(hb, wqkv_ref[:, cols]) + bqkv_ref[:, cols]
        pb = p.astype(BF16)
        taps = [_dot(shift2, pb[r0:r0 + row_len]) for r0 in range(0, tile, row_len)]
        prev = jnp.concatenate([t[0:row_len] for t in taps], axis=0)
        nxt = jnp.concatenate([t[row_len:2 * row_len] for t in taps], axis=0)
        cwj = cw_ref[:, cols]
        s = _silu(prev * cwj[0:1, :] + p * cwj[1:2, :] + nxt * cwj[2:3, :])
        which = (j * PROJ_COLS) // WIDTH_A
        head0 = ((j * PROJ_COLS) % WIDTH_A) // HEAD_DIM
        for hh in range(heads_per_pass):
            sl = s[:, hh * HEAD_DIM:(hh + 1) * HEAD_DIM]
            if which < 2:
                sl = sl * lax.rsqrt(jnp.sum(sl * sl, axis=-1, keepdims=True) + L2_EPS)
            if which == 0:
                sl = sl * (HEAD_DIM ** -0.5)
            outs[which][head0 + hh] = sl.astype(BF16)

    pab = _dot(hb, wab_ref[...]) + bab_ref[...]
    par = par_ref[...]
    z = pab + par[1:2, :]
    softplus = jnp.maximum(z, 0.0) + jnp.log1p(jnp.exp(-jnp.abs(z)))
    g = -jnp.exp(par[0:1, :]) * softplus
    beta = jax.nn.sigmoid(pab)

    g3 = jnp.concatenate(_split3(g), axis=1)
    r = lax.broadcasted_iota(jnp.int32, (GROUP, GROUP), 0)
    c = lax.broadcasted_iota(jnp.int32, (GROUP, GROUP), 1)
    same = (r // CHUNK) == (c // CHUNK)
    tri_f = (same & (r >= c)).astype(BF16)
    tri_b = (same & (r <= c)).astype(BF16)
    lane = lax.broadcasted_iota(jnp.int32, (GROUP, LANES), 1)
    for blk in range(tile // GROUP):
        rows = slice(blk * GROUP, (blk + 1) * GROUP)
        pf = _dot(tri_f, g3[rows])
        pb = _dot(tri_b, g3[rows])
        pf = pf[:, 0:LANES] + pf[:, LANES:2 * LANES] + pf[:, 2 * LANES:3 * LANES]
        pb = pb[:, 0:LANES] + pb[:, LANES:2 * LANES] + pb[:, 2 * LANES:3 * LANES]
        gtot = pf + pb - g[rows]
        gc = jnp.where(lane < N_HEADS, pf, pb)
        vals = jnp.where(
            lane < N_AB, gc,
            jnp.where(lane < 2 * N_AB, beta[rows],
                      jnp.where(lane < 3 * N_AB, pltpu.roll(gtot, 2 * N_AB, 1), 0.0)))
        tr_ref[...] = jnp.transpose(vals)
        for hd in range(N_HEADS):
            gr_ref[hd, blk, 0:3 * N_DIR, :] = tr_ref[pl.ds(hd, 3 * N_DIR, stride=N_HEADS), :]
            gr_ref[hd, blk, 3 * N_DIR:8, :] = jnp.zeros((8 - 3 * N_DIR, GROUP), F32)


def _projection(x, mod3, mod_row, wqkv, bqkv, wab, bab, cw, par, *, tile, row_len):
    b, length, _ = x.shape
    const = lambda shape: pl.BlockSpec(shape, lambda bi, i: (0,) * len(shape),
                                       pipeline_mode=pl.Buffered(1))
    if mod_row is None:
        mod_spec = pl.BlockSpec((None, 3, D_MODEL), lambda bi, i: (bi, 0, 0))
    else:
        mod_spec = pl.BlockSpec((None, 3, D_MODEL), lambda bi, i: (mod_row, 0, 0))
    head_spec = pl.BlockSpec((None, N_HEADS, tile, HEAD_DIM), lambda bi, i: (bi, 0, i, 0))
    head_shape = jax.ShapeDtypeStruct((b, N_HEADS, length, HEAD_DIM), BF16)
    return pl.pallas_call(
        functools.partial(_proj_kernel, tile=tile, row_len=row_len),
        grid=(b, length // tile),
        in_specs=[
            pl.BlockSpec((None, tile, D_MODEL), lambda bi, i: (bi, i, 0)),
            mod_spec,
            const((D_MODEL, 3 * WIDTH_A)),
            const((1, 3 * WIDTH_A)),
            const((D_MODEL, LANES)),
            const((1, LANES)),
            const((3, 3 * WIDTH_A)),
            const((2, LANES)),
        ],
        out_specs=[head_spec, head_spec, head_spec,
                   pl.BlockSpec((None, N_HEADS, tile // GROUP, 8, GROUP),
                                lambda bi, i: (bi, 0, i, 0, 0))],
        out_shape=[head_shape, head_shape, head_shape,
                   jax.ShapeDtypeStruct((b, N_HEADS, length // GROUP, 8, GROUP), F32)],
        scratch_shapes=[pltpu.VMEM((LANES, GROUP), F32)],
        compiler_params=pltpu.CompilerParams(
            dimension_semantics=("arbitrary", "arbitrary"),
            vmem_limit_bytes=48 * MIB),
        name="proj",
    )(x, mod3, wqkv, bqkv, wab, bab, cw, par)


class _Prob(NamedTuple):
    q_ref: object
    k_ref: object
    v_ref: object
    gr_ref: object
    row0: object
    grp: object
    ch0: object
    dirn: int
    need_o: bool


_ROW_GC, _ROW_BETA, _ROW_GTOT = 0, 2, 4


PREP_STAGES = 12


def _gdn_prep(probs, mask_ref, w2q_ref, r_ref, d_ref, o0_ref):
    n = range(len(probs))
    lane = lax.broadcasted_iota(jnp.int32, (1, GROUP), 1)
    chunk_cols = [((lane // CHUNK) == ci).astype(F32) for ci in range(CHUNKS_PER_GROUP)]
    q, kt, kb, v, rows8, gcol, bcol, ecol, a = ([None] * len(probs) for _ in range(9))
    for i, p in enumerate(probs):
        rows = pl.ds(p.row0, GROUP)
        q[i] = p.q_ref[rows, :]
        kf = p.k_ref[rows, :].astype(F32)
        v[i] = p.v_ref[rows, :]
        rows8[i] = p.gr_ref[p.grp]
        cols8 = jnp.transpose(
            jnp.concatenate([rows8[i], jnp.zeros((GROUP - 8, GROUP), F32)], axis=0))
        gcol[i] = cols8[:, _ROW_GC + p.dirn:_ROW_GC + p.dirn + 1]
        bcol[i] = cols8[:, _ROW_BETA + p.dirn:_ROW_BETA + p.dirn + 1]
        ecol[i] = jnp.exp(gcol[i])
        kt[i] = jnp.transpose(kf)
        kb[i] = kf * bcol[i]
        lhs = jnp.concatenate([q[i], kb[i].astype(BF16)], axis=0)
        a[i] = _dot(lhs, kt[i].astype(BF16))
        yield

    qk, mf, t, nb = ([None] * len(probs) for _ in range(4))
    for i, p in enumerate(probs):
        grow = rows8[i][_ROW_GC + p.dirn:_ROW_GC + p.dirn + 1, :]
        decay = jnp.exp((gcol[i] - grow) + mask_ref[p.dirn])
        qk[i] = (a[i][0:GROUP] * decay).astype(BF16)
        mf[i] = a[i][GROUP:2 * GROUP] * decay
        n1 = mf[i] * mask_ref[2]
        nb[i] = n1.astype(BF16)
        t[i] = mask_ref[5] + n1
        yield

    for i in n:
        nb[i] = _dot(nb[i], nb[i]).astype(BF16)
        yield
    for _ in range(2):
        for i in n:
            y = _dot(jnp.concatenate([nb[i], t[i].astype(BF16)], axis=0), nb[i])
            nb[i] = y[0:GROUP].astype(BF16)
            t[i] = t[i] + y[GROUP:2 * GROUP]
            yield
    for i in n:
        t[i] = t[i] + _dot(t[i].astype(BF16), nb[i])
        yield
    tb, xm = [None] * len(probs), [None] * len(probs)
    for lvl in (3, 4):
        for i in n:
            tb[i] = t[i].astype(BF16)
            xm[i] = _dot((mf[i] * mask_ref[lvl]).astype(BF16), tb[i]).astype(BF16)
            yield
        for i in n:
            t[i] = t[i] - _dot(tb[i], xm[i])
            yield

    wu = [None] * len(probs)
    for i in n:
        rhs = jnp.concatenate(
            [(kb[i] * ecol[i]).astype(BF16), (v[i].astype(F32) * bcol[i]).astype(BF16)], axis=1)
        wu[i] = _dot(t[i].astype(BF16), rhs).astype(BF16)
        yield

    for i, p in enumerate(probs):
        gc_row = rows8[i][_ROW_GC + p.dirn:_ROW_GC + p.dirn + 1, :]
        gt_row = rows8[i][_ROW_GTOT + p.dirn:_ROW_GTOT + p.dirn + 1, :]
        kdec_t = kt[i] * jnp.exp(gt_row - gc_row)
        stack = [(kdec_t * chunk_cols[ci]).astype(BF16) for ci in range(CHUNKS_PER_GROUP)]
        if p.need_o:
            stack.append(qk[i])
        y = _dot(jnp.concatenate(stack, axis=0), wu[i])
        for ci in range(CHUNKS_PER_GROUP):
            z = y[ci * HEAD_DIM:(ci + 1) * HEAD_DIM]
            w2q_ref[p.dirn, p.ch0 + ci, 0:HEAD_DIM, :] = z[:, 0:HEAD_DIM].astype(BF16)
            r_ref[p.dirn, p.ch0 + ci] = z[:, HEAD_DIM:2 * HEAD_DIM].astype(BF16)
            d = jnp.exp(gt_row[:, ci * CHUNK:ci * CHUNK + 1])
            d_ref[p.dirn, p.ch0 + ci] = jnp.broadcast_to(d, (8, HEAD_DIM))
        if p.need_o:
            x2 = y[CHUNKS_PER_GROUP * HEAD_DIM:]
            qe = (q[i].astype(F32) * ecol[i] - x2[:, 0:HEAD_DIM]).astype(BF16)
            for ci in range(CHUNKS_PER_GROUP):
                rs = slice(ci * CHUNK, (ci + 1) * CHUNK)
                w2q_ref[p.dirn, p.ch0 + ci, HEAD_DIM:HEAD_DIM + CHUNK, :] = qe[rs]
            o0_ref[p.dirn, pl.ds(p.row0, GROUP), :] = x2[:, HEAD_DIM:2 * HEAD_DIM]
        yield


def _interleave(staged, n_slots, steps, late_steps=(), late_from=0):
    done = 0
    late_done = 0
    slot = 0
    for slot, _ in enumerate(staged, start=1):
        want = slot * len(steps) // n_slots
        while done < want:
            steps[done]()
            done += 1
        if slot > late_from and late_done < len(late_steps):
            late_steps[late_done]()
            late_done += 1
    assert slot == n_slots and done == len(steps) and late_done == len(late_steps)


def _gdn_kernel(qc_ref, kc_ref, vc_ref, grc_ref, qx_ref, kx_ref, vx_ref, grx_ref,
                o_ref, mask_ref, w2q_ref, r_ref, d_ref, o0_ref, state_ref,
                *, n_ctx_chunks, n_lat_chunks, n_seqs):
    step_id = pl.program_id(0)

    @pl.when(step_id == 0)
    def _():
        r = lax.broadcasted_iota(jnp.int32, (GROUP, GROUP), 0)
        c = lax.broadcasted_iota(jnp.int32, (GROUP, GROUP), 1)
        same64 = (r // CHUNK) == (c // CHUNK)
        same32 = (r // 32) == (c // 32)
        same16 = (r // 16) == (c // 16)
        neg_inf = -jnp.inf
        mask_ref[0] = jnp.where(same64 & (r >= c), 0.0, neg_inf)
        mask_ref[1] = jnp.where(same64 & (r <= c), 0.0, neg_inf)
        mask_ref[2] = jnp.where(same16 & (r != c), -1.0, 0.0)
        mask_ref[3] = jnp.where(same32 & jnp.logical_not(same16), 1.0, 0.0)
        mask_ref[4] = jnp.where(jnp.logical_not(same32), 1.0, 0.0)
        mask_ref[5] = jnp.where(r == c, 1.0, 0.0)
        w2q_ref[...] = jnp.zeros(w2q_ref.shape, w2q_ref.dtype)
        r_ref[...] = jnp.zeros(r_ref.shape, r_ref.dtype)
        d_ref[...] = jnp.zeros(d_ref.shape, d_ref.dtype)
        o0_ref[...] = jnp.zeros(o0_ref.shape, o0_ref.dtype)
        state_ref[...] = jnp.zeros(state_ref.shape, state_ref.dtype)

    o0_cur = o0_ref.at[step_id % 2]
    o0_prev = o0_ref.at[(step_id + 1) % 2]
    n_lat_groups = n_lat_chunks // CHUNKS_PER_GROUP
    steps_per_iter = CHUNKS_PER_GROUP * PREP_GROUPS
    n_iter = n_lat_chunks // steps_per_iter

    def lat_probs(it):
        probs = []
        for sub in range(PREP_GROUPS):
            fwd_grp = it * PREP_GROUPS + sub
            for dirn, grp in ((0, fwd_grp), (1, n_lat_groups - 1 - fwd_grp)):
                probs.append(_Prob(qx_ref, kx_ref, vx_ref, grx_ref,
                                   _aligned(grp * GROUP, GROUP), grp,
                                   n_ctx_chunks + grp * CHUNKS_PER_GROUP, dirn, True))
        return probs

    def prep(prob_groups, steps=(), late_steps=()):
        order = []
        for t in range(PREP_STAGES + PREP_LAG * (len(prob_groups) - 1)):
            for g, probs in enumerate(prob_groups):
                if 0 <= t - g * PREP_LAG < PREP_STAGES:
                    order += [g] * len(probs)
        gens = [_gdn_prep(probs, mask_ref, w2q_ref, r_ref, d_ref, o0_cur)
                for probs in prob_groups]

        def staged():
            for g in order:
                next(gens[g])
                yield

        late_from = max(idx for idx, g in enumerate(order) if g == 0) + 1
        _interleave(staged(), len(order), list(steps), list(late_steps), late_from)

    def grouped(probs):
        return [probs[i:i + PREP_WAVE] for i in range(0, len(probs), PREP_WAVE)]

    def scan_step(o0_view, ch_base, n_chunks, i, states):
        new_states = []
        for dirn in range(N_DIR):
            s = states[dirn]
            ci = i if dirn == 0 else n_chunks - 1 - i
            ch = ch_base + ci
            sb = s.astype(BF16)
            if o0_view is not None:
                y = _dot(w2q_ref[dirn, ch], sb)
                row = _aligned(ci * CHUNK, CHUNK)
                o0_view[dirn, pl.ds(row, CHUNK), :] += y[HEAD_DIM:HEAD_DIM + CHUNK]
                y = y[0:HEAD_DIM]
            else:
                y = _dot(w2q_ref[dirn, ch, 0:HEAD_DIM, :], sb)
            new_states.append(d_ref[dirn, ch, 0:1, :] * s + r_ref[dirn, ch].astype(F32) - y)
        return tuple(new_states)

    def lat_scan_steps(o0_view, it, holder):
        def step(j):
            holder[0] = scan_step(o0_view, n_ctx_chunks, n_lat_chunks, it * steps_per_iter + j,
                                  holder[0])
        return [functools.partial(step, j) for j in range(steps_per_iter)]

    ctx_probs = [_Prob(qc_ref, kc_ref, vc_ref, grc_ref, grp * GROUP, grp,
                       grp * CHUNKS_PER_GROUP, dirn, False)
                 for grp in range(n_ctx_chunks // CHUNKS_PER_GROUP) for dirn in range(N_DIR)]
    prev_holder = [(state_ref[0], state_ref[1])]
    zero = jnp.zeros((HEAD_DIM, HEAD_DIM), F32)
    ctx_holder = [(zero, zero)]

    def ctx_step(i):
        ctx_holder[0] = scan_step(None, 0, n_ctx_chunks, i, ctx_holder[0])

    prep([ctx_probs] + grouped(lat_probs(0)),
         lat_scan_steps(o0_prev, n_iter - 1, prev_holder),
         [functools.partial(ctx_step, i) for i in range(n_ctx_chunks)])
    o_ref[...] = (o0_prev[0] + o0_prev[1]).astype(o_ref.dtype)
    ctx_states = ctx_holder[0]

    @pl.when(step_id < n_seqs)
    def _():
        states = ctx_states

        def body(it, states):
            holder = [states]
            prep(grouped(lat_probs(it)), lat_scan_steps(o0_cur, it - 1, holder))
            return holder[0]

        states = lax.fori_loop(1, n_iter, body, states)
        state_ref[0] = states[0]
        state_ref[1] = states[1]


def _gdn(qc, kc, vc, grc, qx, kx, vx, grx):
    b, nh, lc, _ = qc.shape
    lx = qx.shape[2]
    n_ctx_chunks = lc // CHUNK
    n_lat_chunks = lx // CHUNK
    n_chunks = n_ctx_chunks + n_lat_chunks

    n_seqs = b * nh

    def cur(i):
        s = jnp.minimum(i, n_seqs - 1)
        return s // nh, s % nh

    def prev(i):
        s = jnp.maximum(i - 1, 0)
        return s // nh, s % nh

    def seq_spec(length, which=cur):
        return pl.BlockSpec((None, None, length, HEAD_DIM), lambda i: (*which(i), 0, 0))

    def gr_spec(length):
        return pl.BlockSpec((None, None, length // GROUP, 8, GROUP),
                            lambda i: (*cur(i), 0, 0, 0))

    return pl.pallas_call(
        functools.partial(_gdn_kernel, n_ctx_chunks=n_ctx_chunks, n_lat_chunks=n_lat_chunks,
                          n_seqs=n_seqs),
        grid=(n_seqs + 1,),
        in_specs=[seq_spec(lc), seq_spec(lc), seq_spec(lc), gr_spec(lc),
                  seq_spec(lx), seq_spec(lx), seq_spec(lx), gr_spec(lx)],
        out_specs=seq_spec(lx, prev),
        out_shape=jax.ShapeDtypeStruct((b, nh, lx, HEAD_DIM), BF16),
        scratch_shapes=[
            pltpu.VMEM((6, GROUP, GROUP), F32),
            pltpu.VMEM((N_DIR, n_chunks, HEAD_DIM + CHUNK, HEAD_DIM), BF16),
            pltpu.VMEM((N_DIR, n_chunks, HEAD_DIM, HEAD_DIM), BF16),
            pltpu.VMEM((N_DIR, n_chunks, 8, HEAD_DIM), F32),
            pltpu.VMEM((2, N_DIR, lx, HEAD_DIM), F32),
            pltpu.VMEM((N_DIR, HEAD_DIM, HEAD_DIM), F32),
        ],
        compiler_params=pltpu.CompilerParams(
            dimension_semantics=("arbitrary",),
            vmem_limit_bytes=52 * MIB),
        name="gdn",
    )(qc, kc, vc, grc, qx, kx, vx, grx)


def _merge_kernel(x_ref, mod_ref, o_ref, wza_ref, wxin_ref, wbg_ref, wcg_ref, wzb_ref,
                  wga_ref, wgb_ref, wa_ref, wb_ref, wout_ref, bias_ref, onw_ref,
                  cbw_ref, cbb_ref, lng_ref, lnb_ref, out_ref):
    x = x_ref[...]
    m = mod_ref[...]
    h = _layernorm(x) * (1.0 + m[1:2, :]) + m[0:1, :]
    hb = h.astype(BF16)
    bias = bias_ref[...]

    sz = _silu(_dot(hb, wza_ref[...]) + bias[0:1, :])
    onw = onw_ref[...]
    pieces = []
    for hd in range(N_HEADS):
        o = o_ref[hd].astype(F32)
        o = o * lax.rsqrt(jnp.mean(o * o, axis=-1, keepdims=True) + RMS_EPS) * onw
        pieces.append((o * sz[:, hd * HEAD_DIM:(hd + 1) * HEAD_DIM]).astype(BF16))
    ya = jnp.concatenate(pieces, axis=1)
    merged = jax.nn.sigmoid(_dot(hb, wga_ref[...]) + bias[5:6, :]) * _dot(ya, wa_ref[...])

    acc = jnp.zeros((x.shape[0], D_MODEL), F32)
    for j in range(WIDTH_B // MERGE_COLS):
        cols = slice(j * MERGE_COLS, (j + 1) * MERGE_COLS)
        xin = _dot(hb, wxin_ref[:, cols]) + bias[1:2, cols]
        bg = _dot(hb, wbg_ref[:, cols]) + bias[2:3, cols]
        cg = _dot(hb, wcg_ref[:, cols]) + bias[3:4, cols]
        zb = _dot(hb, wzb_ref[:, cols]) + bias[4:5, cols]
        conv = _row_conv3(cg * xin, cbw_ref[:, cols], GRID_W) + cbb_ref[:, cols]
        yb = bg * conv * _silu(zb)
        acc = acc + _dot(yb.astype(BF16), wb_ref[cols, :])
    merged = merged + jax.nn.sigmoid(_dot(hb, wgb_ref[...]) + bias[6:7, :]) * acc

    y = _dot(merged.astype(BF16), wout_ref[...])
    res = DN_ALPHA * x + m[2:3, :] * y
    out_ref[...] = _layernorm(res) * lng_ref[...] + lnb_ref[...]


def _merge(x, mod3, o, weights, bias7, onw, cbw, cbb, lng, lnb):
    b, length, _ = x.shape
    tile = MERGE_TILE
    const = lambda shape: pl.BlockSpec(shape, lambda bi, i: (0,) * len(shape),
                                       pipeline_mode=pl.Buffered(1))
    sq = const((D_MODEL, D_MODEL))
    return pl.pallas_call(
        _merge_kernel,
        grid=(b, length // tile),
        in_specs=[
            pl.BlockSpec((None, tile, D_MODEL), lambda bi, i: (bi, i, 0)),
            pl.BlockSpec((None, 3, D_MODEL), lambda bi, i: (bi, 0, 0)),
            pl.BlockSpec((None, N_HEADS, tile, HEAD_DIM), lambda bi, i: (bi, 0, i, 0)),
        ] + [sq] * 10 + [
            const((7, D_MODEL)),
            const((1, HEAD_DIM)),
            const((3, WIDTH_B)),
            const((1, WIDTH_B)),
            const((1, D_MODEL)),
            const((1, D_MODEL)),
        ],
        out_specs=pl.BlockSpec((None, tile, D_MODEL), lambda bi, i: (bi, i, 0)),
        out_shape=jax.ShapeDtypeStruct((b, length, D_MODEL), F32),
        compiler_params=pltpu.CompilerParams(
            dimension_semantics=("arbitrary", "arbitrary"),
            vmem_limit_bytes=56 * MIB),
        name="merge",
    )(x, mod3, o, *weights, bias7, onw, cbw, cbb, lng, lnb)


def kernel(x, c, ctx, c_ctx, w_mod, b_mod, w_in, b_in, conv_qkv_w, a_log, dt_bias, o_norm_w,
           conv_b_w, conv_b_b, w_a, w_b, w_out, ln_g, ln_b):
    assert w_mod.shape[0] == DEPTH, "single-layer block only"
    batch, seq, _ = x.shape
    ctx_len = ctx.shape[1]
    assert seq % PROJ_TILE == 0 and seq % MERGE_TILE == 0 and ctx_len % GROUP == 0

    c16 = jnp.zeros((16, D_MODEL), F32).at[:batch].set(c).at[batch].set(c_ctx)
    mod = _modulation(c16, w_mod[0], b_mod[0][None, :])
    mod3 = mod.reshape(16, 3, D_MODEL)

    w = w_in[0]
    bi = b_in[0]
    o_qkv, o_za, o_ab = 0, 3 * WIDTH_A, 4 * WIDTH_A
    o_rest = o_ab + 2 * N_AB
    seg = lambda arr, k: arr[..., o_rest + k * 1024:o_rest + (k + 1) * 1024]
    wqkv = w[:, o_qkv:o_za].astype(BF16)
    bqkv = bi[None, o_qkv:o_za]
    wab = jnp.zeros((D_MODEL, LANES), F32).at[:, :2 * N_AB].set(w[:, o_ab:o_rest]).astype(BF16)
    bab = jnp.zeros((1, LANES), F32).at[0, :2 * N_AB].set(bi[o_ab:o_rest])
    par = jnp.zeros((2, LANES), F32)
    par = par.at[0, :N_AB].set(a_log[0].reshape(-1)).at[1, :N_AB].set(dt_bias[0].reshape(-1))
    cw = conv_qkv_w[0]

    qx, kx, vx, grx = _projection(x, mod3, None, wqkv, bqkv, wab, bab, cw, par,
                                  tile=PROJ_TILE, row_len=GRID_W)
    qc, kc, vc, grc = _projection(ctx, mod3, batch, wqkv, bqkv, wab, bab, cw, par,
                                  tile=ctx_len, row_len=ctx_len)
    o = _gdn(qc, kc, vc, grc, qx, kx, vx, grx)

    w_za = w[:, o_za:o_ab].astype(BF16)
    rest_w = [seg(w, k).astype(BF16) for k in range(6)]
    weights = [w_za, rest_w[0], rest_w[1], rest_w[2], rest_w[3], rest_w[4], rest_w[5],
               w_a[0].astype(BF16), w_b[0].astype(BF16), w_out[0].astype(BF16)]
    bias7 = jnp.stack([bi[o_za:o_ab]] + [seg(bi, k) for k in range(6)], axis=0)
    return _merge(x, mod3, o, weights, bias7, o_norm_w[0][None, :], conv_b_w[0],
                  conv_b_b[0][None, :], ln_g[0][None, :], ln_b[0][None, :])
```

```python
import functools
from typing import NamedTuple

import jax
import jax.numpy as jnp
from jax import lax
from jax.experimental import pallas as pl
from jax.experimental.pallas import tpu as pltpu

D_MODEL = 1024
N_HEADS = 8
HEAD_DIM = 128
WIDTH_A = N_HEADS * HEAD_DIM
WIDTH_B = 1024
N_DIR = 2
CHUNK = 64
GRID_W = 64
DEPTH = 1
DN_ALPHA = (2.0 * DEPTH) ** 0.25
LN_EPS = 1e-5
RMS_EPS = 1e-6
L2_EPS = 1e-6

GROUP = 128
CHUNKS_PER_GROUP = GROUP // CHUNK
LANES = 128
N_AB = N_DIR * N_HEADS

PROJ_TILE = 512
PROJ_COLS = 512
MERGE_TILE = 512
MERGE_COLS = 256
MOD_COLS = 512
PREP_GROUPS = 8
PREP_WAVE = 4
PREP_LAG = 0
MIB = 1024 * 1024

F32 = jnp.float32
BF16 = jnp.bfloat16


def _dot(a, b):
    return jnp.dot(a, b, preferred_element_type=F32)


def _aligned(v, m):
    return v if isinstance(v, int) else pl.multiple_of(v, m)


def _split3(a):
    hi = a.astype(BF16)
    r = a - hi.astype(F32)
    mid = r.astype(BF16)
    lo = (r - mid.astype(F32)).astype(BF16)
    return hi, mid, lo


def _silu(z):
    half = 0.5 * z
    return half + half * jnp.tanh(half)


def _layernorm(x):
    mu = jnp.mean(x, axis=-1, keepdims=True)
    xc = x - mu
    var = jnp.mean(xc * xc, axis=-1, keepdims=True)
    return xc * lax.rsqrt(var + LN_EPS)


def _row_conv3(u, w, row_len):
    n = u.shape[0]
    pos = lax.broadcasted_iota(jnp.int32, (n, 1), 0) & (row_len - 1)
    has_prev = (pos != 0).astype(F32)
    has_next = (pos != row_len - 1).astype(F32)
    prev = pltpu.roll(u, 1, 0) * has_prev
    nxt = pltpu.roll(u, n - 1, 0) * has_next
    return prev * w[0:1, :] + u * w[1:2, :] + nxt * w[2:3, :]


def _mod_kernel(c_ref, w_ref, b_ref, o_ref):
    s = _silu(c_ref[...])
    s_hi, s_mid, _ = _split3(s)
    w = w_ref[...]
    w_hi, w_mid, _ = _split3(w)
    o_ref[...] = _dot(s_hi, w_hi) + _dot(s_hi, w_mid) + _dot(s_mid, w_hi) + b_ref[...]


def _modulation(c16, w_mod, b_mod):
    n_out = w_mod.shape[1]
    return pl.pallas_call(
        _mod_kernel,
        grid=(n_out // MOD_COLS,),
        in_specs=[
            pl.BlockSpec((16, D_MODEL), lambda j: (0, 0)),
            pl.BlockSpec((D_MODEL, MOD_COLS), lambda j: (0, j)),
            pl.BlockSpec((1, MOD_COLS), lambda j: (0, j)),
        ],
        out_specs=pl.BlockSpec((16, MOD_COLS), lambda j: (0, j)),
        out_shape=jax.ShapeDtypeStruct((16, n_out), F32),
        name="mod",
    )(c16, w_mod, b_mod)


def _proj_kernel(x_ref, mod_ref, wqkv_ref, bqkv_ref, wab_ref, bab_ref, cw_ref, par_ref,
                 q_ref, k_ref, v_ref, gr_ref, tr_ref, *, tile, row_len):
    m = mod_ref[...]
    h = _layernorm(x_ref[...]) * (1.0 + m[1:2, :]) + m[0:1, :]
    hb = h.astype(BF16)

    sr = lax.broadcasted_iota(jnp.int32, (2 * row_len, row_len), 0)
    sc = lax.broadcasted_iota(jnp.int32, (2 * row_len, row_len), 1)
    src_row = jnp.where(sr < row_len, sr - 1, sr - row_len + 1)
    shift2 = jnp.where(sc == src_row, 1.0, 0.0).astype(BF16)

    heads_per_pass = PROJ_COLS // HEAD_DIM
    outs = (q_ref, k_ref, v_ref)
    for j in range(3 * WIDTH_A // PROJ_COLS):
        cols = slice(j * PROJ_COLS, (j + 1) * PROJ_COLS)
        p = _dot(hb, wqkv_ref[:, cols]) + bqkv_ref[:, cols]
        pb = p.astype(BF16)
        taps = [_dot(shift2, pb[r0:r0 + row_len]) for r0 in range(0, tile, row_len)]
        prev = jnp.concatenate([t[0:row_len] for t in taps], axis=0)
        nxt = jnp.concatenate([t[row_len:2 * row_len] for t in taps], axis=0)
        cwj = cw_ref[:, cols]
        s = _silu(prev * cwj[0:1, :] + p * cwj[1:2, :] + nxt * cwj[2:3, :])
        which = (j * PROJ_COLS) // WIDTH_A
        head0 = ((j * PROJ_COLS) % WIDTH_A) // HEAD_DIM
        for hh in range(heads_per_pass):
            sl = s[:, hh * HEAD_DIM:(hh + 1) * HEAD_DIM]
            if which < 2:
                sl = sl * lax.rsqrt(jnp.sum(sl * sl, axis=-1, keepdims=True) + L2_EPS)
            if which == 0:
                sl = sl * (HEAD_DIM ** -0.5)
            outs[which][head0 + hh] = sl.astype(BF16)

    pab = _dot(hb, wab_ref[...]) + bab_ref[...]
    par = par_ref[...]
    z = pab + par[1:2, :]
    softplus = jnp.maximum(z, 0.0) + jnp.log1p(jnp.exp(-jnp.abs(z)))
    g = -jnp.exp(par[0:1, :]) * softplus
    beta = jax.nn.sigmoid(pab)

    g3 = jnp.concatenate(_split3(g), axis=1)
    r = lax.broadcasted_iota(jnp.int32, (GROUP, GROUP), 0)
    c = lax.broadcasted_iota(jnp.int32, (GROUP, GROUP), 1)
    same = (r // CHUNK) == (c // CHUNK)
    tri_f = (same & (r >= c)).astype(BF16)
    tri_b = (same & (r <= c)).astype(BF16)
    lane = lax.broadcasted_iota(jnp.int32, (GROUP, LANES), 1)
    for blk in range(tile // GROUP):
        rows = slice(blk * GROUP, (blk + 1) * GROUP)
        pf = _dot(tri_f, g3[rows])
        pb = _dot(tri_b, g3[rows])
        pf = pf[:, 0:LANES] + pf[:, LANES:2 * LANES] + pf[:, 2 * LANES:3 * LANES]
        pb = pb[:, 0:LANES] + pb[:, LANES:2 * LANES] + pb[:, 2 * LANES:3 * LANES]
        gtot = pf + pb - g[rows]
        gc = jnp.where(lane < N_HEADS, pf, pb)
        vals = jnp.where(
            lane < N_AB, gc,
            jnp.where(lane < 2 * N_AB, beta[rows],
                      jnp.where(lane < 3 * N_AB, pltpu.roll(gtot, 2 * N_AB, 1), 0.0)))
        tr_ref[...] = jnp.transpose(vals)
        for hd in range(N_HEADS):
            gr_ref[hd, blk, 0:3 * N_DIR, :] = tr_ref[pl.ds(hd, 3 * N_DIR, stride=N_HEADS), :]
            gr_ref[hd, blk, 3 * N_DIR:8, :] = jnp.zeros((8 - 3 * N_DIR, GROUP), F32)


def _projection(x, mod3, mod_row, wqkv, bqkv, wab, bab, cw, par, *, tile, row_len):
    b, length, _ = x.shape
    const = lambda shape: pl.BlockSpec(shape, lambda bi, i: (0,) * len(shape),
                                       pipeline_mode=pl.Buffered(1))
    if mod_row is None:
        mod_spec = pl.BlockSpec((None, 3, D_MODEL), lambda bi, i: (bi, 0, 0))
    else:
        mod_spec = pl.BlockSpec((None, 3, D_MODEL), lambda bi, i: (mod_row, 0, 0))
    head_spec = pl.BlockSpec((None, N_HEADS, tile, HEAD_DIM), lambda bi, i: (bi, 0, i, 0))
    head_shape = jax.ShapeDtypeStruct((b, N_HEADS, length, HEAD_DIM), BF16)
    return pl.pallas_call(
        functools.partial(_proj_kernel, tile=tile, row_len=row_len),
        grid=(b, length // tile),
        in_specs=[
            pl.BlockSpec((None, tile, D_MODEL), lambda bi, i: (bi, i, 0)),
            mod_spec,
            const((D_MODEL, 3 * WIDTH_A)),
            const((1, 3 * WIDTH_A)),
            const((D_MODEL, LANES)),
            const((1, LANES)),
            const((3, 3 * WIDTH_A)),
            const((2, LANES)),
        ],
        out_specs=[head_spec, head_spec, head_spec,
                   pl.BlockSpec((None, N_HEADS, tile // GROUP, 8, GROUP),
                                lambda bi, i: (bi, 0, i, 0, 0))],
        out_shape=[head_shape, head_shape, head_shape,
                   jax.ShapeDtypeStruct((b, N_HEADS, length // GROUP, 8, GROUP), F32)],
        scratch_shapes=[pltpu.VMEM((LANES, GROUP), F32)],
        compiler_params=pltpu.CompilerParams(
            dimension_semantics=("arbitrary", "arbitrary"),
            vmem_limit_bytes=48 * MIB),
        name="proj",
    )(x, mod3, wqkv, bqkv, wab, bab, cw, par)


class _Prob(NamedTuple):
    q_ref: object
    k_ref: object
    v_ref: object
    gr_ref: object
    row0: object
    grp: object
    ch0: object
    dirn: int
    need_o: bool


_ROW_GC, _ROW_BETA, _ROW_GTOT = 0, 2, 4


PREP_STAGES = 12


def _gdn_prep(probs, mask_ref, w2q_ref, r_ref, d_ref, o0_ref):
    n = range(len(probs))
    lane = lax.broadcasted_iota(jnp.int32, (1, GROUP), 1)
    chunk_cols = [((lane // CHUNK) == ci).astype(F32) for ci in range(CHUNKS_PER_GROUP)]
    q, kt, kb, v, rows8, gcol, bcol, ecol, a = ([None] * len(probs) for _ in range(9))
    for i, p in enumerate(probs):
        rows = pl.ds(p.row0, GROUP)
        q[i] = p.q_ref[rows, :]
        kf = p.k_ref[rows, :].astype(F32)
        v[i] = p.v_ref[rows, :]
        rows8[i] = p.gr_ref[p.grp]
        cols8 = jnp.transpose(
            jnp.concatenate([rows8[i], jnp.zeros((GROUP - 8, GROUP), F32)], axis=0))
        gcol[i] = cols8[:, _ROW_GC + p.dirn:_ROW_GC + p.dirn + 1]
        bcol[i] = cols8[:, _ROW_BETA + p.dirn:_ROW_BETA + p.dirn + 1]
        ecol[i] = jnp.exp(gcol[i])
        kt[i] = jnp.transpose(kf)
        kb[i] = kf * bcol[i]
        lhs = jnp.concatenate([q[i], kb[i].astype(BF16)], axis=0)
        a[i] = _dot(lhs, kt[i].astype(BF16))
        yield

    qk, mf, t, nb = ([None] * len(probs) for _ in range(4))
    for i, p in enumerate(probs):
        grow = rows8[i][_ROW_GC + p.dirn:_ROW_GC + p.dirn + 1, :]
        decay = jnp.exp((gcol[i] - grow) + mask_ref[p.dirn])
        qk[i] = (a[i][0:GROUP] * decay).astype(BF16)
        mf[i] = a[i][GROUP:2 * GROUP] * decay
        n1 = mf[i] * mask_ref[2]
        nb[i] = n1.astype(BF16)
        t[i] = mask_ref[5] + n1
        yield

    for i in n:
        nb[i] = _dot(nb[i], nb[i]).astype(BF16)
        yield
    for _ in range(2):
        for i in n:
            y = _dot(jnp.concatenate([nb[i], t[i].astype(BF16)], axis=0), nb[i])
            nb[i] = y[0:GROUP].astype(BF16)
            t[i] = t[i] + y[GROUP:2 * GROUP]
            yield
    for i in n:
        t[i] = t[i] + _dot(t[i].astype(BF16), nb[i])
        yield
    tb, xm = [None] * len(probs), [None] * len(probs)
    for lvl in (3, 4):
        for i in n:
            tb[i] = t[i].astype(BF16)
            xm[i] = _dot((mf[i] * mask_ref[lvl]).astype(BF16), tb[i]).astype(BF16)
            yield
        for i in n:
            t[i] = t[i] - _dot(tb[i], xm[i])
            yield

    wu = [None] * len(probs)
    for i in n:
        rhs = jnp.concatenate(
            [(kb[i] * ecol[i]).astype(BF16), (v[i].astype(F32) * bcol[i]).astype(BF16)], axis=1)
        wu[i] = _dot(t[i].astype(BF16), rhs).astype(BF16)
        yield

    for i, p in enumerate(probs):
        gc_row = rows8[i][_ROW_GC + p.dirn:_ROW_GC + p.dirn + 1, :]
        gt_row = rows8[i][_ROW_GTOT + p.dirn:_ROW_GTOT + p.dirn + 1, :]
        kdec_t = kt[i] * jnp.exp(gt_row - gc_row)
        stack = [(kdec_t * chunk_cols[ci]).astype(BF16) for ci in range(CHUNKS_PER_GROUP)]
        if p.need_o:
            stack.append(qk[i])
        y = _dot(jnp.concatenate(stack, axis=0), wu[i])
        for ci in range(CHUNKS_PER_GROUP):
            z = y[ci * HEAD_DIM:(ci + 1) * HEAD_DIM]
            w2q_ref[p.dirn, p.ch0 + ci, 0:HEAD_DIM, :] = z[:, 0:HEAD_DIM].astype(BF16)
            r_ref[p.dirn, p.ch0 + ci] = z[:, HEAD_DIM:2 * HEAD_DIM].astype(BF16)
            d = jnp.exp(gt_row[:, ci * CHUNK:ci * CHUNK + 1])
            d_ref[p.dirn, p.ch0 + ci] = jnp.broadcast_to(d, (8, HEAD_DIM))
        if p.need_o:
            x2 = y[CHUNKS_PER_GROUP * HEAD_DIM:]
            qe = (q[i].astype(F32) * ecol[i] - x2[:, 0:HEAD_DIM]).astype(BF16)
            for ci in range(CHUNKS_PER_GROUP):
                rs = slice(ci * CHUNK, (ci + 1) * CHUNK)
                w2q_ref[p.dirn, p.ch0 + ci, HEAD_DIM:HEAD_DIM + CHUNK, :] = qe[rs]
            o0_ref[p.dirn, pl.ds(p.row0, GROUP), :] = x2[:, HEAD_DIM:2 * HEAD_DIM]
        yield


def _interleave(staged, n_slots, steps, late_steps=(), late_from=0):
    done = 0
    late_done = 0
    slot = 0
    for slot, _ in enumerate(staged, start=1):
        want = slot * len(steps) // n_slots
        while done < want:
            steps[done]()
            done += 1
        if slot > late_from and late_done < len(late_steps):
            late_steps[late_done]()
            late_done += 1
    assert slot == n_slots and done == len(steps) and late_done == len(late_steps)


def _gdn_kernel(qc_ref, kc_ref, vc_ref, grc_ref, qx_ref, kx_ref, vx_ref, grx_ref,
                o_ref, mask_ref, w2q_ref, r_ref, d_ref, o0_ref, state_ref,
                *, n_ctx_chunks, n_lat_chunks, n_seqs):
    step_id = pl.program_id(0)

    @pl.when(step_id == 0)
    def _():
        r = lax.broadcasted_iota(jnp.int32, (GROUP, GROUP), 0)
        c = lax.broadcasted_iota(jnp.int32, (GROUP, GROUP), 1)
        same64 = (r // CHUNK) == (c // CHUNK)
        same32 = (r // 32) == (c // 32)
        same16 = (r // 16) == (c // 16)
        neg_inf = -jnp.inf
        mask_ref[0] = jnp.where(same64 & (r >= c), 0.0, neg_inf)
        mask_ref[1] = jnp.where(same64 & (r <= c), 0.0, neg_inf)
        mask_ref[2] = jnp.where(same16 & (r != c), -1.0, 0.0)
        mask_ref[3] = jnp.where(same32 & jnp.logical_not(same16), 1.0, 0.0)
        mask_ref[4] = jnp.where(jnp.logical_not(same32), 1.0, 0.0)
        mask_ref[5] = jnp.where(r == c, 1.0, 0.0)
        w2q_ref[...] = jnp.zeros(w2q_ref.shape, w2q_ref.dtype)
        r_ref[...] = jnp.zeros(r_ref.shape, r_ref.dtype)
        d_ref[...] = jnp.zeros(d_ref.shape, d_ref.dtype)
        o0_ref[...] = jnp.zeros(o0_ref.shape, o0_ref.dtype)
        state_ref[...] = jnp.zeros(state_ref.shape, state_ref.dtype)

    o0_cur = o0_ref.at[step_id % 2]
    o0_prev = o0_ref.at[(step_id + 1) % 2]
    n_lat_groups = n_lat_chunks // CHUNKS_PER_GROUP
    steps_per_iter = CHUNKS_PER_GROUP * PREP_GROUPS
    n_iter = n_lat_chunks // steps_per_iter

    def lat_probs(it):
        probs = []
        for sub in range(PREP_GROUPS):
            fwd_grp = it * PREP_GROUPS + sub
            for dirn, grp in ((0, fwd_grp), (1, n_lat_groups - 1 - fwd_grp)):
                probs.append(_Prob(qx_ref, kx_ref, vx_ref, grx_ref,
                                   _aligned(grp * GROUP, GROUP), grp,
                                   n_ctx_chunks + grp * CHUNKS_PER_GROUP, dirn, True))
        return probs

    def prep(prob_groups, steps=(), late_steps=()):
        order = []
        for t in range(PREP_STAGES + PREP_LAG * (len(prob_groups) - 1)):
            for g, probs in enumerate(prob_groups):
                if 0 <= t - g * PREP_LAG < PREP_STAGES:
                    order += [g] * len(probs)
        gens = [_gdn_prep(probs, mask_ref, w2q_ref, r_ref, d_ref, o0_cur)
                for probs in prob_groups]

        def staged():
            for g in order:
                next(gens[g])
                yield

        late_from = max(idx for idx, g in enumerate(order) if g == 0) + 1
        _interleave(staged(), len(order), list(steps), list(late_steps), late_from)

    def grouped(probs):
        return [probs[i:i + PREP_WAVE] for i in range(0, len(probs), PREP_WAVE)]

    def scan_step(o0_view, ch_base, n_chunks, i, states):
        new_states = []
        for dirn in range(N_DIR):
            s = states[dirn]
            ci = i if dirn == 0 else n_chunks - 1 - i
            ch = ch_base + ci
            sb = s.astype(BF16)
            if o0_view is not None:
                y = _dot(w2q_ref[dirn, ch], sb)
                row = _aligned(ci * CHUNK, CHUNK)
                o0_view[dirn, pl.ds(row, CHUNK), :] += y[HEAD_DIM:HEAD_DIM + CHUNK]
                y = y[0:HEAD_DIM]
            else:
                y = _dot(w2q_ref[dirn, ch, 0:HEAD_DIM, :], sb)
            new_states.append(d_ref[dirn, ch, 0:1, :] * s + r_ref[dirn, ch].astype(F32) - y)
        return tuple(new_states)

    def lat_scan_steps(o0_view, it, holder):
        def step(j):
            holder[0] = scan_step(o0_view, n_ctx_chunks, n_lat_chunks, it * steps_per_iter + j,
                                  holder[0])
        return [functools.partial(step, j) for j in range(steps_per_iter)]

    ctx_probs = [_Prob(qc_ref, kc_ref, vc_ref, grc_ref, grp * GROUP, grp,
                       grp * CHUNKS_PER_GROUP, dirn, False)
                 for grp in range(n_ctx_chunks // CHUNKS_PER_GROUP) for dirn in range(N_DIR)]
    prev_holder = [(state_ref[0], state_ref[1])]
    zero = jnp.zeros((HEAD_DIM, HEAD_DIM), F32)
    ctx_holder = [(zero, zero)]

    def ctx_step(i):
        ctx_holder[0] = scan_step(None, 0, n_ctx_chunks, i, ctx_holder[0])

    prep([ctx_probs] + grouped(lat_probs(0)),
         lat_scan_steps(o0_prev, n_iter - 1, prev_holder),
         [functools.partial(ctx_step, i) for i in range(n_ctx_chunks)])
    o_ref[...] = (o0_prev[0] + o0_prev[1]).astype(o_ref.dtype)
    ctx_states = ctx_holder[0]

    @pl.when(step_id < n_seqs)
    def _():
        states = ctx_states

        def body(it, states):
            holder = [states]
            prep(grouped(lat_probs(it)), lat_scan_steps(o0_cur, it - 1, holder))
            return holder[0]

        states = lax.fori_loop(1, n_iter, body, states)
        state_ref[0] = states[0]
        state_ref[1] = states[1]


def _gdn(qc, kc, vc, grc, qx, kx, vx, grx):
    b, nh, lc, _ = qc.shape
    lx = qx.shape[2]
    n_ctx_chunks = lc // CHUNK
    n_lat_chunks = lx // CHUNK
    n_chunks = n_ctx_chunks + n_lat_chunks

    n_seqs = b * nh

    def cur(i):
        s = jnp.minimum(i, n_seqs - 1)
        return s // nh, s % nh

    def prev(i):
        s = jnp.maximum(i - 1, 0)
        return s // nh, s % nh

    def seq_spec(length, which=cur):
        return pl.BlockSpec((None, None, length, HEAD_DIM), lambda i: (*which(i), 0, 0))

    def gr_spec(length):
        return pl.BlockSpec((None, None, length // GROUP, 8, GROUP),
                            lambda i: (*cur(i), 0, 0, 0))

    return pl.pallas_call(
        functools.partial(_gdn_kernel, n_ctx_chunks=n_ctx_chunks, n_lat_chunks=n_lat_chunks,
                          n_seqs=n_seqs),
        grid=(n_seqs + 1,),
        in_specs=[seq_spec(lc), seq_spec(lc), seq_spec(lc), gr_spec(lc),
                  seq_spec(lx), seq_spec(lx), seq_spec(lx), gr_spec(lx)],
        out_specs=seq_spec(lx, prev),
        out_shape=jax.ShapeDtypeStruct((b, nh, lx, HEAD_DIM), BF16),
        scratch_shapes=[
            pltpu.VMEM((6, GROUP, GROUP), F32),
            pltpu.VMEM((N_DIR, n_chunks, HEAD_DIM + CHUNK, HEAD_DIM), BF16),
            pltpu.VMEM((N_DIR, n_chunks, HEAD_DIM, HEAD_DIM), BF16),
            pltpu.VMEM((N_DIR, n_chunks, 8, HEAD_DIM), F32),
            pltpu.VMEM((2, N_DIR, lx, HEAD_DIM), F32),
            pltpu.VMEM((N_DIR, HEAD_DIM, HEAD_DIM), F32),
        ],
        compiler_params=pltpu.CompilerParams(
            dimension_semantics=("arbitrary",),
            vmem_limit_bytes=52 * MIB),
        name="gdn",
    )(qc, kc, vc, grc, qx, kx, vx, grx)


def _merge_kernel(x_ref, mod_ref, o_ref, wza_ref, wxin_ref, wbg_ref, wcg_ref, wzb_ref,
                  wga_ref, wgb_ref, wa_ref, wb_ref, wout_ref, bias_ref, onw_ref,
                  cbw_ref, cbb_ref, lng_ref, lnb_ref, out_ref):
    x = x_ref[...]
    m = mod_ref[...]
    h = _layernorm(x) * (1.0 + m[1:2, :]) + m[0:1, :]
    hb = h.astype(BF16)
    bias = bias_ref[...]

    sz = _silu(_dot(hb, wza_ref[...]) + bias[0:1, :])
    onw = onw_ref[...]
    pieces = []
    for hd in range(N_HEADS):
        o = o_ref[hd].astype(F32)
        o = o * lax.rsqrt(jnp.mean(o * o, axis=-1, keepdims=True) + RMS_EPS) * onw
        pieces.append((o * sz[:, hd * HEAD_DIM:(hd + 1) * HEAD_DIM]).astype(BF16))
    ya = jnp.concatenate(pieces, axis=1)
    merged = jax.nn.sigmoid(_dot(hb, wga_ref[...]) + bias[5:6, :]) * _dot(ya, wa_ref[...])

    acc = jnp.zeros((x.shape[0], D_MODEL), F32)
    for j in range(WIDTH_B // MERGE_COLS):
        cols = slice(j * MERGE_COLS, (j + 1) * MERGE_COLS)
        xin = _dot(hb, wxin_ref[:, cols]) + bias[1:2, cols]
        bg = _dot(hb, wbg_ref[:, cols]) + bias[2:3, cols]
        cg = _dot(hb, wcg_ref[:, cols]) + bias[3:4, cols]
        zb = _dot(hb, wzb_ref[:, cols]) + bias[4:5, cols]
        conv = _row_conv3(cg * xin, cbw_ref[:, cols], GRID_W) + cbb_ref[:, cols]
        yb = bg * conv * _silu(zb)
        acc = acc + _dot(yb.astype(BF16), wb_ref[cols, :])
    merged = merged + jax.nn.sigmoid(_dot(hb, wgb_ref[...]) + bias[6:7, :]) * acc

    y = _dot(merged.astype(BF16), wout_ref[...])
    res = DN_ALPHA * x + m[2:3, :] * y
    out_ref[...] = _layernorm(res) * lng_ref[...] + lnb_ref[...]


def _merge(x, mod3, o, weights, bias7, onw, cbw, cbb, lng, lnb):
    b, length, _ = x.shape
    tile = MERGE_TILE
    const = lambda shape: pl.BlockSpec(shape, lambda bi, i: (0,) * len(shape),
                                       pipeline_mode=pl.Buffered(1))
    sq = const((D_MODEL, D_MODEL))
    return pl.pallas_call(
        _merge_kernel,
        grid=(b, length // tile),
        in_specs=[
            pl.BlockSpec((None, tile, D_MODEL), lambda bi, i: (bi, i, 0)),
            pl.BlockSpec((None, 3, D_MODEL), lambda bi, i: (bi, 0, 0)),
            pl.BlockSpec((None, N_HEADS, tile, HEAD_DIM), lambda bi, i: (bi, 0, i, 0)),
        ] + [sq] * 10 + [
            const((7, D_MODEL)),
            const((1, HEAD_DIM)),
            const((3, WIDTH_B)),
            const((1, WIDTH_B)),
            const((1, D_MODEL)),
            const((1, D_MODEL)),
        ],
        out_specs=pl.BlockSpec((None, tile, D_MODEL), lambda bi, i: (bi, i, 0)),
        out_shape=jax.ShapeDtypeStruct((b, length, D_MODEL), F32),
        compiler_params=pltpu.CompilerParams(
            dimension_semantics=("arbitrary", "arbitrary"),
            vmem_limit_bytes=56 * MIB),
        name="merge",
    )(x, mod3, o, *weights, bias7, onw, cbw, cbb, lng, lnb)


def kernel(x, c, ctx, c_ctx, w_mod, b_mod, w_in, b_in, conv_qkv_w, a_log, dt_bias, o_norm_w,
           conv_b_w, conv_b_b, w_a, w_b, w_out, ln_g, ln_b):
    assert w_mod.shape[0] == DEPTH, "single-layer block only"
    batch, seq, _ = x.shape
    ctx_len = ctx.shape[1]
    assert seq % PROJ_TILE == 0 and seq % MERGE_TILE == 0 and ctx_len % GROUP == 0

    c16 = jnp.zeros((16, D_MODEL), F32).at[:batch].set(c).at[batch].set(c_ctx)
    mod = _modulation(c16, w_mod[0], b_mod[0][None, :])
    mod3 = mod.reshape(16, 3, D_MODEL)

    w = w_in[0]
    bi = b_in[0]
    o_qkv, o_za, o_ab = 0, 3 * WIDTH_A, 4 * WIDTH_A
    o_rest = o_ab + 2 * N_AB
    seg = lambda arr, k: arr[..., o_rest + k * 1024:o_rest + (k + 1) * 1024]
    wqkv = w[:, o_qkv:o_za].astype(BF16)
    bqkv = bi[None, o_qkv:o_za]
    wab = jnp.zeros((D_MODEL, LANES), F32).at[:, :2 * N_AB].set(w[:, o_ab:o_rest]).astype(BF16)
    bab = jnp.zeros((1, LANES), F32).at[0, :2 * N_AB].set(bi[o_ab:o_rest])
    par = jnp.zeros((2, LANES), F32)
    par = par.at[0, :N_AB].set(a_log[0].reshape(-1)).at[1, :N_AB].set(dt_bias[0].reshape(-1))
    cw = conv_qkv_w[0]

    qx, kx, vx, grx = _projection(x, mod3, None, wqkv, bqkv, wab, bab, cw, par,
                                  tile=PROJ_TILE, row_len=GRID_W)
    qc, kc, vc, grc = _projection(ctx, mod3, batch, wqkv, bqkv, wab, bab, cw, par,
                                  tile=ctx_len, row_len=ctx_len)
    o = _gdn(qc, kc, vc, grc, qx, kx, vx, grx)

    w_za = w[:, o_za:o_ab].astype(BF16)
    rest_w = [seg(w, k).astype(BF16) for k in range(6)]
    weights = [w_za, rest_w[0], rest_w[1], rest_w[2], rest_w[3], rest_w[4], rest_w[5],
               w_a[0].astype(BF16), w_b[0].astype(BF16), w_out[0].astype(BF16)]
    bias7 = jnp.stack([bi[o_za:o_ab]] + [seg(bi, k) for k in range(6)], axis=0)
    return _merge(x, mod3, o, weights, bias7, o_norm_w[0][None, :], conv_b_w[0],
                  conv_b_b[0][None, :], ln_g[0][None, :], ln_b[0][None, :])
```
